```python
import jax, jax.numpy as jnp
from jax import lax
import numpy as np

D_MODEL = 1024
BATCH = 32
SEQ = 2048
DEPTH = 4

GM_WIDTH = D_MODEL
GM_CHUNK = 128
GM_GROUPS = 8
GM_GROUP_DIM = GM_WIDTH // GM_GROUPS
ML_WIDTH = D_MODEL
ML_HEADS = 4
ML_HEAD_DIM = ML_WIDTH // ML_HEADS
ML_CHUNK = 128
ML_CONV = 5
F_BIAS_INIT = 3.0
IN_COLS = 2 * GM_WIDTH + 2 * ML_WIDTH + 2 * D_MODEL
SPLITS = [GM_WIDTH, 2 * GM_WIDTH, 2 * GM_WIDTH + ML_WIDTH, 2 * GM_WIDTH + 2 * ML_WIDTH,
          2 * GM_WIDTH + 2 * ML_WIDTH + D_MODEL]
PEER_HEADS = 8
PEER_KEYS = 128
PEER_EXPERTS = PEER_KEYS * PEER_KEYS
PEER_TOPK = 16
PEER_QDIM = 256
PEER_HALF = PEER_QDIM // 2
PEER_TOKENS = 128
EPS = 1e-6
NEG = -1e30

kernel_name = "hybrid_gmlp_mlstm_peer_encoder"


def rmsnorm(x, g):
    xf = x.astype(jnp.float32)
    y = xf * lax.rsqrt(jnp.mean(xf * xf, axis=-1, keepdims=True) + EPS)
    return (y * g.astype(jnp.float32)).astype(x.dtype)


def gmlp_spatial_gate(u, v, g_v, w_s, b_s):
    B, S, _ = v.shape
    v = rmsnorm(v, g_v).reshape(B, S // GM_CHUNK, GM_CHUNK, GM_GROUPS, GM_GROUP_DIM)
    mixed = jnp.einsum('gts,bcsgd->bctgd', w_s, v) + b_s.T[None, None, :, :, None]
    return u * mixed.reshape(B, S, GM_WIDTH)


def centred_dwconv(x, w, b):
    pad = ML_CONV // 2
    y = lax.conv_general_dilated(x, w[:, None, :], window_strides=(1,), padding=[(pad, pad)],
                                 dimension_numbers=('NWC', 'WIO', 'NWC'),
                                 feature_group_count=x.shape[-1])
    return y + b


def mlstm_chunkwise(q, k, v, log_i, log_f):
    B, H, S, dh = q.shape
    L = ML_CHUNK
    nc = S // L

    def to_chunks(a):
        return jnp.moveaxis(a.reshape((B, H, nc, L) + a.shape[3:]), 2, 0)

    xs = tuple(to_chunks(a) for a in (q, k, v, log_i, log_f))
    tri = jnp.tril(jnp.ones((L, L), dtype=bool))

    def step(carry, chunk):
        C, n, m = carry
        qt, kt, vt, it, ft = chunk
        b = jnp.cumsum(ft, axis=-1)
        d = jnp.where(tri, b[..., :, None] - b[..., None, :] + it[..., None, :], NEG)
        inter = b + m[..., None]
        m_t = jnp.maximum(inter, jnp.max(d, axis=-1))
        w_intra = jnp.exp(d - m_t[..., None])
        w_inter = jnp.exp(inter - m_t)
        s = jnp.einsum('bhtd,bhsd->bhts', qt, kt) * w_intra
        num = jnp.einsum('bhts,bhsd->bhtd', s, vt) + w_inter[..., None] * jnp.einsum('bhvk,bhtk->bhtv', C, qt)
        den = jnp.sum(s, axis=-1) + w_inter * jnp.einsum('bhk,bhtk->bht', n, qt)
        h = num / jnp.maximum(jnp.abs(den), jnp.exp(-m_t))[..., None]
        bL = b[..., -1]
        g_s = bL[..., None] - b + it
        m_new = jnp.maximum(bL + m, jnp.max(g_s, axis=-1))
        ws = jnp.exp(g_s - m_new[..., None])
        wc = jnp.exp(bL + m - m_new)
        C_new = wc[..., None, None] * C + jnp.einsum('bhs,bhsv,bhsk->bhvk', ws, vt, kt)
        n_new = wc[..., None] * n + jnp.einsum('bhs,bhsk->bhk', ws, kt)
        return (C_new, n_new, m_new), h

    init = (jnp.zeros((B, H, dh, dh), jnp.float32), jnp.zeros((B, H, dh), jnp.float32),
            jnp.zeros((B, H), jnp.float32))
    _, hs = lax.scan(step, init, xs)
    return jnp.moveaxis(hs, 0, 2).reshape(B, H, S, dh)


def mlstm_branch(xm, o_pre, conv_w, conv_b, w_q, w_k, w_v, w_gate, b_gate, head_g, skip):
    B, S, _ = xm.shape
    xc = jax.nn.silu(centred_dwconv(xm, conv_w, conv_b))
    heads = lambda a: a.reshape(B, S, ML_HEADS, ML_HEAD_DIM)
    q = jnp.einsum('bshd,hde->bhse', heads(xc), w_q).astype(jnp.float32)
    k = (jnp.einsum('bshd,hde->bhse', heads(xc), w_k) * ML_HEAD_DIM ** -0.5).astype(jnp.float32)
    v = jnp.einsum('bshd,hde->bhse', heads(xm), w_v).astype(jnp.float32)
    gates = (xc @ w_gate + b_gate).astype(jnp.float32).reshape(B, S, 4, ML_HEADS)
    i_f, f_f, i_b, f_b = jnp.transpose(gates, (2, 0, 3, 1))
    h_fwd = mlstm_chunkwise(q, k, v, i_f, jax.nn.log_sigmoid(f_f))
    flip = lambda a: jnp.flip(a, axis=2)
    h_bwd = flip(mlstm_chunkwise(flip(q), flip(k), flip(v), flip(i_b), flip(jax.nn.log_sigmoid(f_b))))
    h = h_fwd + h_bwd
    h = h * lax.rsqrt(jnp.mean(h * h, axis=-1, keepdims=True) + EPS) * head_g.astype(jnp.float32)[:, None, :]
    h = jnp.transpose(h, (0, 2, 1, 3)).reshape(B, S, ML_WIDTH).astype(xm.dtype)
    return jax.nn.sigmoid(o_pre) * (h + skip * xc)


def hybrid_mixer(h, w_in, gm_norm_g, gm_w_s, gm_b_s, ml_conv_w, ml_conv_b, ml_w_q, ml_w_k, ml_w_v,
                 ml_w_gate, ml_b_gate, ml_head_g, ml_skip, w_branch_a, w_branch_b, w_out):
    z = jnp.einsum('bsd,dc->bsc', h, w_in)
    z_u, z_v, xm, o_pre, g_a, g_b = jnp.split(z, SPLITS, axis=-1)
    y_a = gmlp_spatial_gate(jax.nn.gelu(z_u), jax.nn.gelu(z_v), gm_norm_g, gm_w_s, gm_b_s)
    y_b = mlstm_branch(xm, o_pre, ml_conv_w, ml_conv_b, ml_w_q, ml_w_k, ml_w_v, ml_w_gate, ml_b_gate,
                       ml_head_g, ml_skip)
    merged = jax.nn.sigmoid(g_a) * (y_a @ w_branch_a) + jax.nn.sigmoid(g_b) * (y_b @ w_branch_b)
    return merged @ w_out


def peer_ffn(x, w_query, sub_keys, expert_u, expert_v):
    B, S, D = x.shape
    xt = x.reshape((B * S) // PEER_TOKENS, PEER_TOKENS, D)

    def block(xb):
        n = xb.shape[0]
        q = (xb @ w_query).reshape(n, PEER_HEADS, 2, PEER_HALF)
        s = jnp.einsum('nhpc,hpkc->nhpk', q, sub_keys).astype(jnp.float32)
        sv, si = lax.top_k(s, PEER_TOPK)
        cand = (sv[:, :, 0, :, None] + sv[:, :, 1, None, :]).reshape(n, PEER_HEADS, PEER_TOPK * PEER_TOPK)
        cv, ci = lax.top_k(cand, PEER_TOPK)
        i1 = jnp.take_along_axis(si[:, :, 0], ci // PEER_TOPK, axis=-1)
        i2 = jnp.take_along_axis(si[:, :, 1], ci % PEER_TOPK, axis=-1)
        idx = i1 * PEER_KEYS + i2
        g = jax.nn.softmax(cv, axis=-1).astype(xb.dtype)
        act = jax.nn.gelu(jnp.einsum('nhkd,nd->nhk', expert_u[idx], xb))
        return jnp.einsum('nhk,nhkd->nd', g * act, expert_v[idx])

    return lax.map(block, xt).reshape(B, S, D)


def setup_inputs(seed: int = 0) -> dict:
    key = jax.random.key(seed)
    ks = iter(jax.random.split(key, 32))
    L = DEPTH

    def nrm(shape, scale):
        return jax.random.normal(next(ks), shape, jnp.float32) * scale

    gate_b = nrm((L, 4, ML_HEADS), 0.1) + jnp.array([0.0, F_BIAS_INIT, 0.0, F_BIAS_INIT], jnp.float32)[None, :, None]
    return {
        "x": nrm((BATCH, SEQ, D_MODEL), 1.0),
        "norm_mix_g": 1.0 + nrm((L, D_MODEL), 0.02),
        "w_in": nrm((L, D_MODEL, IN_COLS), D_MODEL ** -0.5),
        "gm_norm_g": 1.0 + nrm((L, GM_WIDTH), 0.02),
        "gm_w_s": nrm((L, GM_GROUPS, GM_CHUNK, GM_CHUNK), GM_CHUNK ** -0.5),
        "gm_b_s": 1.0 + nrm((L, GM_GROUPS, GM_CHUNK), 0.1),
        "ml_conv_w": nrm((L, ML_CONV, ML_WIDTH), ML_CONV ** -0.5),
        "ml_conv_b": nrm((L, ML_WIDTH), 0.02),
        "ml_w_q": nrm((L, ML_HEADS, ML_HEAD_DIM, ML_HEAD_DIM), ML_HEAD_DIM ** -0.5),
        "ml_w_k": nrm((L, ML_HEADS, ML_HEAD_DIM, ML_HEAD_DIM), ML_HEAD_DIM ** -0.5),
        "ml_w_v": nrm((L, ML_HEADS, ML_HEAD_DIM, ML_HEAD_DIM), ML_HEAD_DIM ** -0.5),
        "ml_w_gate": nrm((L, ML_WIDTH, 4 * ML_HEADS), ML_WIDTH ** -0.5),
        "ml_b_gate": gate_b.reshape(L, 4 * ML_HEADS),
        "ml_head_g": 1.0 + nrm((L, ML_HEADS, ML_HEAD_DIM), 0.02),
        "ml_skip": 1.0 + nrm((L, ML_WIDTH), 0.02),
        "w_branch_a": nrm((L, GM_WIDTH, D_MODEL), GM_WIDTH ** -0.5),
        "w_branch_b": nrm((L, ML_WIDTH, D_MODEL), ML_WIDTH ** -0.5),
        "w_out": nrm((L, D_MODEL, D_MODEL), D_MODEL ** -0.5),
        "norm_ffn_g": 1.0 + nrm((L, D_MODEL), 0.02),
        "peer_w_query": nrm((L, D_MODEL, PEER_HEADS * PEER_QDIM), D_MODEL ** -0.5),
        "peer_sub_keys": nrm((L, PEER_HEADS, 2, PEER_KEYS, PEER_HALF), PEER_HALF ** -0.5),
        "peer_u": nrm((L, PEER_EXPERTS, D_MODEL), D_MODEL ** -0.5),
        "peer_v": nrm((L, PEER_EXPERTS, D_MODEL), (PEER_HEADS * PEER_TOPK) ** -0.5),
        "final_g": 1.0 + nrm((D_MODEL,), 0.02),
    }


def reference(x, norm_mix_g, w_in, gm_norm_g, gm_w_s, gm_b_s, ml_conv_w, ml_conv_b, ml_w_q, ml_w_k,
              ml_w_v, ml_w_gate, ml_b_gate, ml_head_g, ml_skip, w_branch_a, w_branch_b, w_out,
              norm_ffn_g, peer_w_query, peer_sub_keys, peer_u, peer_v, final_g):
    for l in range(DEPTH):
        x = x + hybrid_mixer(rmsnorm(x, norm_mix_g[l]), w_in[l], gm_norm_g[l], gm_w_s[l], gm_b_s[l],
                             ml_conv_w[l], ml_conv_b[l], ml_w_q[l], ml_w_k[l], ml_w_v[l], ml_w_gate[l],
                             ml_b_gate[l], ml_head_g[l], ml_skip[l], w_branch_a[l], w_branch_b[l], w_out[l])
        x = x + peer_ffn(rmsnorm(x, norm_ffn_g[l]), peer_w_query[l], peer_sub_keys[l], peer_u[l], peer_v[l])
    return rmsnorm(x, final_g)
```

```python
import functools

import jax
import jax.numpy as jnp
from jax import lax
from jax.experimental import pallas as pl
from jax.experimental.pallas import tpu as pltpu
from jax.experimental.pallas import tpu_sc as plsc

F32 = jnp.float32
BF16 = jnp.bfloat16
I32 = jnp.int32

EPS = 1e-6
NEG = -1e30
CHUNK = 128
GM_GROUPS = 8
ML_HEADS = 4
ML_CONV = 5
CONV_PAD = 8
PEER_HEADS = 8
PEER_TOPK = 16
N_IN_SEG = 6
GELU_C = 0.7978845608028654

VMEM_LIMIT = 56 * 1024 * 1024


def _cparams(*sem):
    return pltpu.CompilerParams(dimension_semantics=sem, vmem_limit_bytes=VMEM_LIMIT)


def _rows(c, n=CHUNK):
    return pl.ds(pl.multiple_of(c * n, n), n)


def _rms(x, g):
    return x * lax.rsqrt(jnp.mean(x * x, axis=-1, keepdims=True) + EPS) * g


def _dot(a, b):
    return jnp.dot(a, b, preferred_element_type=F32)


def _dot_nt(a, b):
    return lax.dot_general(a, b, (((1,), (1,)), ((), ())), preferred_element_type=F32)


def _dot_tn(a, b):
    return lax.dot_general(a, b, (((0,), (0,)), ((), ())), preferred_element_type=F32)


def _in_proj_kernel(x_ref, g_ref, w_ref, gmg_ref, ws_ref, bst_ref, out_ref, hn_s, gu_s):
    j = pl.program_id(1)
    n_chunks = x_ref.shape[1] // CHUNK
    d = x_ref.shape[2]

    def z_of(c):
        return _dot(hn_s[_rows(c), :], w_ref[...])

    def for_chunks(fn):
        def body(c, carry):
            fn(c)
            return carry
        lax.fori_loop(0, n_chunks, body, 0)

    @pl.when(j == 0)
    def _():
        def norm(c):
            hn_s[_rows(c), :] = _rms(x_ref[0, _rows(c), :], g_ref[...]).astype(BF16)
        for_chunks(norm)

        def seg_u(c):
            gu_s[_rows(c), :] = jax.nn.gelu(z_of(c)).astype(BF16)
        for_chunks(seg_u)

    @pl.when(j == 1)
    def _():
        def seg_v(c):
            vn = _rms(jax.nn.gelu(z_of(c)), gmg_ref[...]).astype(BF16)
            gu = gu_s[_rows(c), :]
            for g in range(GM_GROUPS):
                cs = slice(g * (d // GM_GROUPS), (g + 1) * (d // GM_GROUPS))
                mixed = _dot(ws_ref[g], vn[:, cs]) + bst_ref[:, g:g + 1]
                out_ref[0, 0, _rows(c), cs] = (gu[:, cs].astype(F32) * mixed).astype(BF16)
        for_chunks(seg_v)

    @pl.when(j == 2)
    def _():
        def seg_xm(c):
            out_ref[0, 0, _rows(c), :] = z_of(c).astype(BF16)
        for_chunks(seg_xm)

    @pl.when(j >= 3)
    def _():
        def seg_gate(c):
            out_ref[0, 0, _rows(c), :] = jax.nn.sigmoid(z_of(c)).astype(BF16)
        for_chunks(seg_gate)


def _in_proj(x3, g, w_in, gm_g, w_s, b_st):
    b, s, d = x3.shape
    return pl.pallas_call(
        _in_proj_kernel,
        grid=(b, N_IN_SEG),
        in_specs=[
            pl.BlockSpec((1, s, d), lambda i, j: (i, 0, 0)),
            pl.BlockSpec((1, d), lambda i, j: (0, 0)),
            pl.BlockSpec((d, d), lambda i, j: (0, j)),
            pl.BlockSpec((1, d), lambda i, j: (0, 0)),
            pl.BlockSpec((GM_GROUPS, CHUNK, CHUNK), lambda i, j: (0, 0, 0)),
            pl.BlockSpec((CHUNK, GM_GROUPS), lambda i, j: (0, 0)),
        ],
        out_specs=pl.BlockSpec((1, 1, s, d), lambda i, j: (jnp.maximum(j - 1, 0), i, 0, 0)),
        out_shape=jax.ShapeDtypeStruct((N_IN_SEG - 1, b, s, d), BF16),
        scratch_shapes=[pltpu.VMEM((s, d), BF16), pltpu.VMEM((s, d), BF16)],
        compiler_params=_cparams("arbitrary", "arbitrary"),
        name="in_proj",
    )(x3, g, w_in, gm_g, w_s, b_st)


def _split3(a):
    hi = a.astype(BF16)
    r1 = a - hi.astype(F32)
    mid = r1.astype(BF16)
    lo = (r1 - mid.astype(F32)).astype(BF16)
    return hi, mid, lo


def _log_sigmoid(x):
    return jnp.minimum(x, 0.0) - jnp.log(1.0 + jnp.exp(-jnp.abs(x)))


def _mlstm_kernel(xm_ref, so_ref, cw_ref, cb_ref, wq_ref, wk_ref, wv_ref, wg_ref, wgt_ref,
                  bg_ref, bgt_ref, hg_ref, sk_ref, out_ref,
                  pad_s, xc_s, gc_s, gr_s, q_s, k_s, v_s, hf_s, ct_s):
    s = xm_ref.shape[2]
    d = xm_ref.shape[3]
    dh = d // ML_HEADS
    n_chunks = s // CHUNK
    nh = ML_HEADS

    ri = lax.broadcasted_iota(I32, (CHUNK, CHUNK), 0)
    ci = lax.broadcasted_iota(I32, (CHUNK, CHUNK), 1)
    tril = ri >= ci
    triu = ri <= ci
    tril_b = jnp.where(tril, 1.0, 0.0).astype(BF16)
    triu_b = jnp.where(triu, 1.0, 0.0).astype(BF16)

    pad_s[0:CONV_PAD, :] = jnp.zeros((CONV_PAD, d), F32)
    pad_s[s + CONV_PAD:s + 2 * CONV_PAD, :] = jnp.zeros((CONV_PAD, d), F32)

    def fill(c, carry):
        pad_s[pl.ds(pl.multiple_of(c * CHUNK, CHUNK) + CONV_PAD, CHUNK), :] = (
            xm_ref[0, 0, _rows(c), :].astype(F32))
        return carry
    lax.fori_loop(0, n_chunks, fill, 0)

    lane = lax.broadcasted_iota(I32, (s, CHUNK), 1)
    v_s[:, dh:dh + CHUNK] = jnp.where(lane == 0, 1.0, 0.0).astype(BF16)

    def conv_gates(c, carry):
        base = pl.multiple_of(c * CHUNK, CHUNK)
        gc = jnp.zeros((CHUNK, 4 * nh), F32) + bg_ref[...]
        gr = jnp.zeros((4 * nh, CHUNK), F32) + bgt_ref[...]
        for hh in range(nh):
            cs = slice(hh * dh, (hh + 1) * dh)
            acc = jnp.zeros((CHUNK, dh), F32) + cb_ref[:, cs]
            win = pad_s[pl.ds(base, CHUNK + 2 * CONV_PAD), cs]
            for t in range(ML_CONV):
                off = CONV_PAD - ML_CONV // 2 + t
                acc = acc + win[off:off + CHUNK, :] * cw_ref[t:t + 1, cs]
            xc = (acc * jax.nn.sigmoid(acc)).astype(BF16)
            xc_s[_rows(c), cs] = xc
            gc = gc + _dot(xc, wg_ref[cs, :])
            gr = gr + _dot_nt(wgt_ref[:, cs], xc)
        lf_c = _log_sigmoid(gc[:, nh:2 * nh])
        lb_c = _log_sigmoid(gc[:, 3 * nh:4 * nh])
        lf_r = _log_sigmoid(gr[nh:2 * nh, :])
        lb_r = _log_sigmoid(gr[3 * nh:4 * nh, :])
        bcf = sum(_dot(tril_b, p) for p in _split3(lf_c))
        bcb = sum(_dot(triu_b, p) for p in _split3(lb_c))
        brf = sum(_dot(p, triu_b) for p in _split3(lf_r))
        brb = sum(_dot(p, tril_b) for p in _split3(lb_r))
        gc_s[_rows(c), 0:nh] = gc[:, 0:nh]
        gc_s[_rows(c), nh:2 * nh] = bcf
        gc_s[_rows(c), 2 * nh:3 * nh] = gc[:, 2 * nh:3 * nh]
        gc_s[_rows(c), 3 * nh:4 * nh] = bcb
        gr_s[0:nh, _rows(c)] = gr[0:nh, :]
        gr_s[nh:2 * nh, _rows(c)] = brf
        gr_s[2 * nh:3 * nh, _rows(c)] = gr[2 * nh:3 * nh, :]
        gr_s[3 * nh:4 * nh, _rows(c)] = brb
        return carry
    lax.fori_loop(0, n_chunks, conv_gates, 0)

    for hh in range(nh):
        cs = slice(hh * dh, (hh + 1) * dh)

        def qkv(c, carry):
            xc = xc_s[_rows(c), cs]
            q_s[_rows(c), :] = _dot(xc, wq_ref[hh]).astype(BF16)
            k_s[_rows(c), :] = (_dot(xc, wk_ref[hh]) * (dh ** -0.5)).astype(BF16)
            v_s[_rows(c), 0:dh] = _dot(xm_ref[0, 0, _rows(c), cs], wv_ref[hh]).astype(BF16)
            return carry
        lax.fori_loop(0, n_chunks, qkv, 0)

        def step(c, m, i_col, b_col, mask, last_row):
            r = _rows(c)
            qc, kc, vx = q_s[r, :], k_s[r, :], v_s[r, :]
            bc = gc_s[r, b_col:b_col + 1]
            ic = gc_s[r, i_col:i_col + 1]
            br = gr_s[b_col:b_col + 1, r]
            ir = gr_s[i_col:i_col + 1, r]
            dm = jnp.where(mask, bc - br + ir, NEG)
            inter = bc + m
            m_t = jnp.maximum(inter, jnp.max(dm, axis=-1, keepdims=True))
            w_intra = jnp.exp(dm - m_t)
            w_inter = jnp.exp(inter - m_t)
            sc = (_dot_nt(qc, kc) * w_intra).astype(BF16)
            res = _dot(sc, vx) + w_inter * _dot(qc, ct_s[...].astype(BF16))
            num = res[:, 0:dh]
            den = res[:, dh:dh + 1]
            h = num / jnp.maximum(jnp.abs(den), jnp.exp(-m_t))
            b_l = bc[last_row:last_row + 1, :]
            g_col = b_l - bc + ic
            m_new = jnp.maximum(b_l + m, jnp.max(g_col, axis=0, keepdims=True))
            w_s = jnp.exp(g_col - m_new)
            w_c = jnp.exp(b_l + m - m_new)
            ct_s[...] = w_c * ct_s[...] + _dot_tn(kc, (w_s * vx.astype(F32)).astype(BF16))
            return h, m_new

        ct_s[...] = jnp.zeros(ct_s.shape, F32)

        def fwd(c, m):
            h, m_new = step(c, m, hh, nh + hh, tril, CHUNK - 1)
            hf_s[_rows(c), :] = h
            return m_new
        lax.fori_loop(0, n_chunks, fwd, jnp.zeros((1, 1), F32))

        ct_s[...] = jnp.zeros(ct_s.shape, F32)

        def bwd(i, m):
            c = n_chunks - 1 - i
            h, m_new = step(c, m, 2 * nh + hh, 3 * nh + hh, triu, 0)
            h = h + hf_s[_rows(c), :]
            y = _rms(h, hg_ref[:, cs])
            xc = xc_s[_rows(c), cs].astype(F32)
            so = so_ref[0, 0, _rows(c), cs].astype(F32)
            out_ref[0, _rows(c), cs] = (so * (y + sk_ref[:, cs] * xc)).astype(BF16)
            return m_new
        lax.fori_loop(0, n_chunks, bwd, jnp.zeros((1, 1), F32))


def _mlstm(z_all, cw, cb, wq, wk, wv, wg, wgt, bg, bgt, hg, sk):
    _, b, s, d = z_all.shape
    dh = d // ML_HEADS
    ng = 4 * ML_HEADS

    def full(a):
        return pl.BlockSpec(a.shape, lambda i, _n=a.ndim: (0,) * _n)

    return pl.pallas_call(
        _mlstm_kernel,
        grid=(b,),
        in_specs=[
            pl.BlockSpec((1, 1, s, d), lambda i: (1, i, 0, 0)),
            pl.BlockSpec((1, 1, s, d), lambda i: (2, i, 0, 0)),
            full(cw), full(cb), full(wq), full(wk), full(wv), full(wg), full(wgt),
            full(bg), full(bgt), full(hg), full(sk),
        ],
        out_specs=pl.BlockSpec((1, s, d), lambda i: (i, 0, 0)),
        out_shape=jax.ShapeDtypeStruct((b, s, d), BF16),
        scratch_shapes=[
            pltpu.VMEM((s + 2 * CONV_PAD, d), F32),
            pltpu.VMEM((s, d), BF16),
            pltpu.VMEM((s, ng), F32),
            pltpu.VMEM((ng, s), F32),
            pltpu.VMEM((s, dh), BF16),
            pltpu.VMEM((s, dh), BF16),
            pltpu.VMEM((s, dh + CHUNK), BF16),
            pltpu.VMEM((s, dh), F32),
            pltpu.VMEM((dh, dh + CHUNK), F32),
        ],
        compiler_params=_cparams("arbitrary"),
        name="mlstm",
    )(z_all, z_all, cw, cb, wq, wk, wv, wg, wgt, bg, bgt, hg, sk)


def _merge_kernel(x_ref, ya_ref, yb_ref, ga_ref, gb_ref, wa_ref, wb_ref, wo_ref, out_ref):
    merged = (ga_ref[0].astype(F32) * _dot(ya_ref[0], wa_ref[...])
              + gb_ref[0].astype(F32) * _dot(yb_ref[...], wb_ref[...]))
    out_ref[...] = x_ref[...] + _dot(merged.astype(BF16), wo_ref[...])


def _merge(x2, z_all, yb2, wa, wb, wo, tm):
    t, d = x2.shape
    z3 = z_all.reshape(z_all.shape[0], t, d)

    def slab(k):
        return pl.BlockSpec((1, tm, d), lambda i, _k=k: (_k, i, 0))

    row = pl.BlockSpec((tm, d), lambda i: (i, 0))
    wspec = pl.BlockSpec((d, d), lambda i: (0, 0))
    return pl.pallas_call(
        _merge_kernel,
        grid=(t // tm,),
        in_specs=[row, slab(0), row, slab(3), slab(4), wspec, wspec, wspec],
        out_specs=row,
        out_shape=jax.ShapeDtypeStruct((t, d), F32),
        compiler_params=_cparams("arbitrary"),
        name="merge",
    )(x2, z3, yb2, z3, z3, wa, wb, wo)


def _peer_route_kernel(x_ref, g_ref, wq_ref, keys_ref, xn_ref, idx_ref, gw_ref,
                       q_s, sv_s, si_s, cv_s, ce_s, it_s, gt_s):
    tm = x_ref.shape[0]
    k = PEER_TOPK
    big = 1e9

    xn = _rms(x_ref[...], g_ref[...])
    xn_ref[...] = xn
    q_s[...] = _dot(xn.astype(BF16), wq_ref[...]).astype(BF16)

    def sub_topk(hp, carry):
        qh = q_s[:, pl.ds(pl.multiple_of(hp * CHUNK, CHUNK), CHUNK)]
        sc = _dot_nt(keys_ref[hp], qh)
        rowf = lax.broadcasted_iota(I32, sc.shape, 0).astype(F32)
        for j in range(k):
            m = jnp.max(sc, axis=0, keepdims=True)
            i = jnp.min(jnp.where(sc == m, rowf, big), axis=0, keepdims=True)
            sv_s[hp, j:j + 1, :] = m
            si_s[hp, j:j + 1, :] = i
            sc = jnp.where(rowf == i, -jnp.inf, sc)
        return carry
    lax.fori_loop(0, 2 * PEER_HEADS, sub_topk, 0)

    def head_topk(h, carry):
        sv0, sv1 = sv_s[2 * h], sv_s[2 * h + 1]
        si0, si1 = si_s[2 * h], si_s[2 * h + 1]
        cand = jnp.concatenate([sv0[a:a + 1, :] + sv1 for a in range(k)], axis=0)
        eid = jnp.concatenate([si0[a:a + 1, :] * float(CHUNK) + si1 for a in range(k)], axis=0)
        rowf = lax.broadcasted_iota(I32, cand.shape, 0).astype(F32)
        for j in range(k):
            m = jnp.max(cand, axis=0, keepdims=True)
            i = jnp.min(jnp.where(cand == m, rowf, big), axis=0, keepdims=True)
            hit = rowf == i
            cv_s[j:j + 1, :] = m
            ce_s[j:j + 1, :] = jnp.sum(jnp.where(hit, eid, 0.0), axis=0, keepdims=True)
            cand = jnp.where(hit, -jnp.inf, cand)
        cv = cv_s[...]
        e = jnp.exp(cv - cv[0:1, :])
        rows = pl.ds(pl.multiple_of(h * k, k), k)
        gt_s[rows, :] = e / jnp.sum(e, axis=0, keepdims=True)
        it_s[rows, :] = ce_s[...]
        return carry
    lax.fori_loop(0, PEER_HEADS, head_topk, 0)

    idx_ref[...] = it_s[...].T.astype(I32)
    gw_ref[...] = gt_s[...].T


def _peer_route(x2, g, wq, keys, tm):
    t, d = x2.shape
    qd = wq.shape[1]
    nsel = PEER_HEADS * PEER_TOPK
    row = pl.BlockSpec((tm, d), lambda i: (i, 0))
    sel = pl.BlockSpec((tm, nsel), lambda i: (i, 0))
    return pl.pallas_call(
        _peer_route_kernel,
        grid=(t // tm,),
        in_specs=[
            row,
            pl.BlockSpec((1, d), lambda i: (0, 0)),
            pl.BlockSpec((d, qd), lambda i: (0, 0)),
            pl.BlockSpec(keys.shape, lambda i: (0, 0, 0)),
        ],
        out_specs=[row, sel, sel],
        out_shape=[
            jax.ShapeDtypeStruct((t, d), F32),
            jax.ShapeDtypeStruct((t, nsel), I32),
            jax.ShapeDtypeStruct((t, nsel), F32),
        ],
        scratch_shapes=[
            pltpu.VMEM((tm, qd), BF16),
            pltpu.VMEM((2 * PEER_HEADS, PEER_TOPK, tm), F32),
            pltpu.VMEM((2 * PEER_HEADS, PEER_TOPK, tm), F32),
            pltpu.VMEM((PEER_TOPK, tm), F32),
            pltpu.VMEM((PEER_TOPK, tm), F32),
            pltpu.VMEM((nsel, tm), F32),
            pltpu.VMEM((nsel, tm), F32),
        ],
        compiler_params=_cparams("arbitrary"),
        name="peer_route",
    )(x2, g, wq, keys)


SC_LANES = 16
SC_TOK_BLOCK = 8


def _peer_experts(u_tab, v_tab, idx, gw, xn, xres):
    t, d = xn.shape
    k = PEER_TOPK
    nh = PEER_HEADS
    assert k == SC_LANES
    info = plsc.get_sparse_core_info()
    nc, ns = info.num_cores, info.num_subcores
    nw = nc * ns
    tb = SC_TOK_BLOCK
    assert t % (nw * tb) == 0 and d % SC_LANES == 0
    tpw = t // nw
    n_lane_chunks = d // SC_LANES
    mesh = plsc.VectorSubcoreMesh(core_axis_name="c", subcore_axis_name="s")
    idx2 = idx.reshape(t * nh, k)
    gw2 = gw.reshape(t * nh, k)

    @functools.partial(
        pl.kernel,
        mesh=mesh,
        out_type=jax.ShapeDtypeStruct((t, d), F32),
        scratch_types=[
            pltpu.VMEM((tb * nh, k), I32),
            pltpu.VMEM((tb * nh, k), F32),
            pltpu.VMEM((tb, d), F32),
            pltpu.VMEM((tb, d), F32),
            pltpu.VMEM((2, k, d), F32),
            pltpu.VMEM((2, k, d), F32),
            pltpu.VMEM((k, SC_LANES), F32),
            pltpu.VMEM((k,), F32),
            pltpu.SemaphoreType.DMA((2,)),
            pltpu.SemaphoreType.DMA((2,)),
        ],
        compiler_params=pltpu.CompilerParams(needs_layout_passes=False),
        name="peer_experts",
    )
    def run(u_hbm, v_hbm, idx_hbm, gw_hbm, xn_hbm, xres_hbm, out_hbm,
            idx_v, gw_v, xn_v, out_v, ub, vb, red_v, w_v, usem, vsem):
        wid = lax.axis_index("s") * nc + lax.axis_index("c")
        tok0 = wid * tpw
        lane = lax.iota(I32, SC_LANES)

        @pl.loop(0, tpw // tb)
        def _(blk):
            base = tok0 + blk * tb
            pltpu.sync_copy(idx_hbm.at[pl.ds(base * nh, tb * nh)], idx_v)
            pltpu.sync_copy(gw_hbm.at[pl.ds(base * nh, tb * nh)], gw_v)
            pltpu.sync_copy(xn_hbm.at[pl.ds(base, tb)], xn_v)
            pltpu.sync_copy(xres_hbm.at[pl.ds(base, tb)], out_v)

            @pl.loop(0, tb)
            def _(tok):
                def start(h):
                    row = tok * nh + h
                    slot = h % 2
                    cu = pltpu.async_copy(u_hbm.at[idx_v.at[row]], ub.at[slot], usem.at[slot])
                    cv = pltpu.async_copy(v_hbm.at[idx_v.at[row]], vb.at[slot], vsem.at[slot])
                    return cu, cv

                copies = [None] * nh
                copies[0] = start(0)
                for h in range(nh):
                    slot = h % 2
                    if h + 1 < nh:
                        copies[h + 1] = start(h + 1)
                    copies[h][0].wait()

                    def dot_rows(c, accs):
                        off = pl.multiple_of(c * SC_LANES, SC_LANES)
                        xv = xn_v[tok, pl.ds(off, SC_LANES)]
                        return tuple(accs[r] + ub[slot, r, pl.ds(off, SC_LANES)] * xv
                                     for r in range(k))
                    accs = lax.fori_loop(
                        0, n_lane_chunks, dot_rows,
                        tuple(jnp.zeros((SC_LANES,), F32) for _ in range(k)))
                    for r in range(k):
                        red_v[r, :] = accs[r]
                    act = jnp.zeros((SC_LANES,), F32)
                    for jj in range(SC_LANES):
                        act = act + plsc.load_gather(red_v, [lane, jnp.full((SC_LANES,), jj, I32)])
                    z = GELU_C * (act + 0.044715 * act * act * act)
                    tanh_z = 1.0 - 2.0 / (jnp.exp(2.0 * z) + 1.0)
                    w_v[...] = gw_v[tok * nh + h, :] * (0.5 * act * (1.0 + tanh_z))
                    ws = [plsc.load_gather(w_v, [jnp.full((SC_LANES,), r, I32)]) for r in range(k)]

                    copies[h][1].wait()

                    def mix_rows(c, carry):
                        off = pl.multiple_of(c * SC_LANES, SC_LANES)
                        ps = [vb[slot, r, pl.ds(off, SC_LANES)] * ws[r] for r in range(k)]
                        while len(ps) > 1:
                            ps = [ps[i] + ps[i + 1] for i in range(0, len(ps), 2)]
                        out_v[tok, pl.ds(off, SC_LANES)] = out_v[tok, pl.ds(off, SC_LANES)] + ps[0]
                        return carry
                    lax.fori_loop(0, n_lane_chunks, mix_rows, 0)

            pltpu.sync_copy(out_v, out_hbm.at[pl.ds(base, tb)])

    return run(u_tab, v_tab, idx2, gw2, xn, xres)


def _final_norm_kernel(x_ref, g_ref, out_ref):
    out_ref[...] = _rms(x_ref[...], g_ref[...])


def _final_norm(x2, g, tm):
    t, d = x2.shape
    row = pl.BlockSpec((tm, d), lambda i: (i, 0))
    return pl.pallas_call(
        _final_norm_kernel,
        grid=(t // tm,),
        in_specs=[row, pl.BlockSpec((1, d), lambda i: (0, 0))],
        out_specs=row,
        out_shape=jax.ShapeDtypeStruct((t, d), F32),
        compiler_params=_cparams("arbitrary"),
        name="final_norm",
    )(x2, g)


def _row_tile(t, want):
    tm = min(want, t)
    assert t % tm == 0
    return tm


def kernel(x, norm_mix_g, w_in, gm_norm_g, gm_w_s, gm_b_s, ml_conv_w, ml_conv_b, ml_w_q, ml_w_k, ml_w_v, ml_w_gate, ml_b_gate, ml_head_g, ml_skip, w_branch_a, w_branch_b, w_out, norm_ffn_g, peer_w_query, peer_sub_keys, peer_u, peer_v, final_g):
    b, s, d = x.shape
    depth = w_in.shape[0]
    t = b * s
    assert s % CHUNK == 0 and d % (ML_HEADS * CHUNK) == 0 and w_in.shape[2] == N_IN_SEG * d
    tm_merge = _row_tile(t, 512)
    tm_route = _row_tile(t, 256)
    x2 = x.reshape(t, d)
    for l in range(depth):
        z_all = _in_proj(
            x2.reshape(b, s, d), norm_mix_g[l].reshape(1, d), w_in[l].astype(BF16),
            gm_norm_g[l].reshape(1, d), gm_w_s[l].astype(BF16), gm_b_s[l].T)
        wg = ml_w_gate[l].astype(BF16)
        yb = _mlstm(
            z_all, ml_conv_w[l], ml_conv_b[l].reshape(1, d),
            ml_w_q[l].astype(BF16), ml_w_k[l].astype(BF16), ml_w_v[l].astype(BF16),
            wg, wg.T, ml_b_gate[l].reshape(1, -1), ml_b_gate[l].reshape(-1, 1),
            ml_head_g[l].reshape(1, d), ml_skip[l].reshape(1, d))
        x2 = _merge(x2, z_all, yb.reshape(t, d), w_branch_a[l].astype(BF16),
                    w_branch_b[l].astype(BF16), w_out[l].astype(BF16), tm_merge)
        keys = peer_sub_keys[l].reshape(2 * PEER_HEADS, CHUNK, -1).astype(BF16)
        xn, idx, gw = _peer_route(x2, norm_ffn_g[l].reshape(1, d),
                                  peer_w_query[l].astype(BF16), keys, tm_route)
        x2 = _peer_experts(peer_u[l], peer_v[l], idx, gw, xn, x2)
    return _final_norm(x2, final_g.reshape(1, d), tm_merge).reshape(b, s, d)
```

```python
import functools

import jax
import jax.numpy as jnp
from jax import lax
from jax.experimental import pallas as pl
from jax.experimental.pallas import tpu as pltpu
from jax.experimental.pallas import tpu_sc as plsc

F32 = jnp.float32
BF16 = jnp.bfloat16
I32 = jnp.int32

EPS = 1e-6
NEG = -1e30
CHUNK = 128
GM_GROUPS = 8
ML_HEADS = 4
ML_CONV = 5
CONV_PAD = 8
PEER_HEADS = 8
PEER_TOPK = 16
N_IN_SEG = 6
BATCH_GROUPS = 2
GELU_C = 0.7978845608028654

VMEM_LIMIT = 56 * 1024 * 1024


def _cparams(*sem):
    return pltpu.CompilerParams(dimension_semantics=sem, vmem_limit_bytes=VMEM_LIMIT)


def _rows(c, n=CHUNK):
    return pl.ds(pl.multiple_of(c * n, n), n)


def _rms(x, g):
    return x * lax.rsqrt(jnp.mean(x * x, axis=-1, keepdims=True) + EPS) * g


def _dot(a, b):
    return jnp.dot(a, b, preferred_element_type=F32)


def _dot_nt(a, b):
    return lax.dot_general(a, b, (((1,), (1,)), ((), ())), preferred_element_type=F32)


def _dot_tn(a, b):
    return lax.dot_general(a, b, (((0,), (0,)), ((), ())), preferred_element_type=F32)


def _in_proj_kernel(x_ref, g_ref, w_ref, gmg_ref, ws_ref, bst_ref, out_ref, hn_s, gu_s):
    j = pl.program_id(1)
    n_chunks = x_ref.shape[1] // CHUNK
    d = x_ref.shape[2]

    def z_of(c):
        return _dot(hn_s[_rows(c), :], w_ref[...])

    def for_chunks(fn):
        def body(c, carry):
            fn(c)
            return carry
        lax.fori_loop(0, n_chunks, body, 0)

    @pl.when(j == 0)
    def _():
        def norm(c):
            hn_s[_rows(c), :] = _rms(x_ref[0, _rows(c), :], g_ref[...]).astype(BF16)
        for_chunks(norm)

        def seg_u(c):
            gu_s[_rows(c), :] = jax.nn.gelu(z_of(c)).astype(BF16)
        for_chunks(seg_u)

    @pl.when(j == 1)
    def _():
        def seg_v(c):
            vn = _rms(jax.nn.gelu(z_of(c)), gmg_ref[...]).astype(BF16)
            gu = gu_s[_rows(c), :]
            for g in range(GM_GROUPS):
                cs = slice(g * (d // GM_GROUPS), (g + 1) * (d // GM_GROUPS))
                mixed = _dot(ws_ref[g], vn[:, cs]) + bst_ref[:, g:g + 1]
                out_ref[0, 0, _rows(c), cs] = (gu[:, cs].astype(F32) * mixed).astype(BF16)
        for_chunks(seg_v)

    @pl.when(j == 2)
    def _():
        def seg_xm(c):
            out_ref[0, 0, _rows(c), :] = z_of(c).astype(BF16)
        for_chunks(seg_xm)

    @pl.when(j >= 3)
    def _():
        def seg_gate(c):
            out_ref[0, 0, _rows(c), :] = jax.nn.sigmoid(z_of(c)).astype(BF16)
        for_chunks(seg_gate)


def _in_proj(x3, g, w_in, gm_g, w_s, b_st):
    b, s, d = x3.shape
    return pl.pallas_call(
        _in_proj_kernel,
        grid=(b, N_IN_SEG),
        in_specs=[
            pl.BlockSpec((1, s, d), lambda i, j: (i, 0, 0)),
            pl.BlockSpec((1, d), lambda i, j: (0, 0)),
            pl.BlockSpec((d, d), lambda i, j: (0, j)),
            pl.BlockSpec((1, d), lambda i, j: (0, 0)),
            pl.BlockSpec((GM_GROUPS, CHUNK, CHUNK), lambda i, j: (0, 0, 0)),
            pl.BlockSpec((CHUNK, GM_GROUPS), lambda i, j: (0, 0)),
        ],
        out_specs=pl.BlockSpec((1, 1, s, d), lambda i, j: (jnp.maximum(j - 1, 0), i, 0, 0)),
        out_shape=jax.ShapeDtypeStruct((N_IN_SEG - 1, b, s, d), BF16),
        scratch_shapes=[pltpu.VMEM((s, d), BF16), pltpu.VMEM((s, d), BF16)],
        compiler_params=_cparams("arbitrary", "arbitrary"),
        name="in_proj",
    )(x3, g, w_in, gm_g, w_s, b_st)


def _split3(a):
    hi = a.astype(BF16)
    r1 = a - hi.astype(F32)
    mid = r1.astype(BF16)
    lo = (r1 - mid.astype(F32)).astype(BF16)
    return hi, mid, lo


def _log_sigmoid(x):
    return jnp.minimum(x, 0.0) - jnp.log(1.0 + jnp.exp(-jnp.abs(x)))


def _mlstm_kernel(xm_ref, so_ref, cw_ref, cb_ref, wq_ref, wk_ref, wv_ref, wg_ref, wgt_ref,
                  bg_ref, bgt_ref, hg_ref, sk_ref, out_ref,
                  pad_s, xc_s, gc_s, gr_s, q_s, k_s, v_s, hf_s, ct_s):
    s = xm_ref.shape[2]
    d = xm_ref.shape[3]
    dh = d // ML_HEADS
    n_chunks = s // CHUNK
    nh = ML_HEADS

    ri = lax.broadcasted_iota(I32, (CHUNK, CHUNK), 0)
    ci = lax.broadcasted_iota(I32, (CHUNK, CHUNK), 1)
    tril = ri >= ci
    triu = ri <= ci
    tril_b = jnp.where(tril, 1.0, 0.0).astype(BF16)
    triu_b = jnp.where(triu, 1.0, 0.0).astype(BF16)

    pad_s[0:CONV_PAD, :] = jnp.zeros((CONV_PAD, d), F32)
    pad_s[s + CONV_PAD:s + 2 * CONV_PAD, :] = jnp.zeros((CONV_PAD, d), F32)

    def fill(c, carry):
        pad_s[pl.ds(pl.multiple_of(c * CHUNK, CHUNK) + CONV_PAD, CHUNK), :] = (
            xm_ref[0, 0, _rows(c), :].astype(F32))
        return carry
    lax.fori_loop(0, n_chunks, fill, 0)

    lane = lax.broadcasted_iota(I32, (s, CHUNK), 1)
    v_s[:, dh:dh + CHUNK] = jnp.where(lane == 0, 1.0, 0.0).astype(BF16)

    def conv_gates(c, carry):
        base = pl.multiple_of(c * CHUNK, CHUNK)
        gc = jnp.zeros((CHUNK, 4 * nh), F32) + bg_ref[...]
        gr = jnp.zeros((4 * nh, CHUNK), F32) + bgt_ref[...]
        for hh in range(nh):
            cs = slice(hh * dh, (hh + 1) * dh)
            acc = jnp.zeros((CHUNK, dh), F32) + cb_ref[:, cs]
            win = pad_s[pl.ds(base, CHUNK + 2 * CONV_PAD), cs]
            for t in range(ML_CONV):
                off = CONV_PAD - ML_CONV // 2 + t
                acc = acc + win[off:off + CHUNK, :] * cw_ref[t:t + 1, cs]
            xc = (acc * jax.nn.sigmoid(acc)).astype(BF16)
            xc_s[_rows(c), cs] = xc
            gc = gc + _dot(xc, wg_ref[cs, :])
            gr = gr + _dot_nt(wgt_ref[:, cs], xc)
        lf_c = _log_sigmoid(gc[:, nh:2 * nh])
        lb_c = _log_sigmoid(gc[:, 3 * nh:4 * nh])
        lf_r = _log_sigmoid(gr[nh:2 * nh, :])
        lb_r = _log_sigmoid(gr[3 * nh:4 * nh, :])
        bcf = sum(_dot(tril_b, p) for p in _split3(lf_c))
        bcb = sum(_dot(triu_b, p) for p in _split3(lb_c))
        brf = sum(_dot(p, triu_b) for p in _split3(lf_r))
        brb = sum(_dot(p, tril_b) for p in _split3(lb_r))
        gc_s[_rows(c), 0:nh] = gc[:, 0:nh]
        gc_s[_rows(c), nh:2 * nh] = bcf
        gc_s[_rows(c), 2 * nh:3 * nh] = gc[:, 2 * nh:3 * nh]
        gc_s[_rows(c), 3 * nh:4 * nh] = bcb
        gr_s[0:nh, _rows(c)] = gr[0:nh, :]
        gr_s[nh:2 * nh, _rows(c)] = brf
        gr_s[2 * nh:3 * nh, _rows(c)] = gr[2 * nh:3 * nh, :]
        gr_s[3 * nh:4 * nh, _rows(c)] = brb
        return carry
    lax.fori_loop(0, n_chunks, conv_gates, 0)

    for hh in range(nh):
        cs = slice(hh * dh, (hh + 1) * dh)

        def qkv(c, carry):
            xc = xc_s[_rows(c), cs]
            q_s[_rows(c), :] = _dot(xc, wq_ref[hh]).astype(BF16)
            k_s[_rows(c), :] = (_dot(xc, wk_ref[hh]) * (dh ** -0.5)).astype(BF16)
            v_s[_rows(c), 0:dh] = _dot(xm_ref[0, 0, _rows(c), cs], wv_ref[hh]).astype(BF16)
            return carry
        lax.fori_loop(0, n_chunks, qkv, 0)

        def step(c, m, i_col, b_col, mask, last_row):
            r = _rows(c)
            qc, kc, vx = q_s[r, :], k_s[r, :], v_s[r, :]
            bc = gc_s[r, b_col:b_col + 1]
            ic = gc_s[r, i_col:i_col + 1]
            br = gr_s[b_col:b_col + 1, r]
            ir = gr_s[i_col:i_col + 1, r]
            dm = jnp.where(mask, bc - br + ir, NEG)
            inter = bc + m
            m_t = jnp.maximum(inter, jnp.max(dm, axis=-1, keepdims=True))
            w_intra = jnp.exp(dm - m_t)
            w_inter = jnp.exp(inter - m_t)
            sc = (_dot_nt(qc, kc) * w_intra).astype(BF16)
            res = _dot(sc, vx) + w_inter * _dot(qc, ct_s[...].astype(BF16))
            num = res[:, 0:dh]
            den = res[:, dh:dh + 1]
            h = num / jnp.maximum(jnp.abs(den), jnp.exp(-m_t))
            b_l = bc[last_row:last_row + 1, :]
            g_col = b_l - bc + ic
            m_new = jnp.maximum(b_l + m, jnp.max(g_col, axis=0, keepdims=True))
            w_s = jnp.exp(g_col - m_new)
            w_c = jnp.exp(b_l + m - m_new)
            ct_s[...] = w_c * ct_s[...] + _dot_tn(kc, (w_s * vx.astype(F32)).astype(BF16))
            return h, m_new

        ct_s[...] = jnp.zeros(ct_s.shape, F32)

        def fwd(c, m):
            h, m_new = step(c, m, hh, nh + hh, tril, CHUNK - 1)
            hf_s[_rows(c), :] = h
            return m_new
        lax.fori_loop(0, n_chunks, fwd, jnp.zeros((1, 1), F32))

        ct_s[...] = jnp.zeros(ct_s.shape, F32)

        def bwd(i, m):
            c = n_chunks - 1 - i
            h, m_new = step(c, m, 2 * nh + hh, 3 * nh + hh, triu, 0)
            h = h + hf_s[_rows(c), :]
            y = _rms(h, hg_ref[:, cs])
            xc = xc_s[_rows(c), cs].astype(F32)
            so = so_ref[0, 0, _rows(c), cs].astype(F32)
            out_ref[0, _rows(c), cs] = (so * (y + sk_ref[:, cs] * xc)).astype(BF16)
            return m_new
        lax.fori_loop(0, n_chunks, bwd, jnp.zeros((1, 1), F32))


def _mlstm(z_all, cw, cb, wq, wk, wv, wg, wgt, bg, bgt, hg, sk):
    _, b, s, d = z_all.shape
    dh = d // ML_HEADS
    ng = 4 * ML_HEADS

    def full(a):
        return pl.BlockSpec(a.shape, lambda i, _n=a.ndim: (0,) * _n)

    return pl.pallas_call(
        _mlstm_kernel,
        grid=(b,),
        in_specs=[
            pl.BlockSpec((1, 1, s, d), lambda i: (1, i, 0, 0)),
            pl.BlockSpec((1, 1, s, d), lambda i: (2, i, 0, 0)),
            full(cw), full(cb), full(wq), full(wk), full(wv), full(wg), full(wgt),
            full(bg), full(bgt), full(hg), full(sk),
        ],
        out_specs=pl.BlockSpec((1, s, d), lambda i: (i, 0, 0)),
        out_shape=jax.ShapeDtypeStruct((b, s, d), BF16),
        scratch_shapes=[
            pltpu.VMEM((s + 2 * CONV_PAD, d), F32),
            pltpu.VMEM((s, d), BF16),
            pltpu.VMEM((s, ng), F32),
            pltpu.VMEM((ng, s), F32),
            pltpu.VMEM((s, dh), BF16),
            pltpu.VMEM((s, dh), BF16),
            pltpu.VMEM((s, dh + CHUNK), BF16),
            pltpu.VMEM((s, dh), F32),
            pltpu.VMEM((dh, dh + CHUNK), F32),
        ],
        compiler_params=_cparams("arbitrary"),
        name="mlstm",
    )(z_all, z_all, cw, cb, wq, wk, wv, wg, wgt, bg, bgt, hg, sk)


def _merge_kernel(x_ref, ya_ref, yb_ref, ga_ref, gb_ref, wa_ref, wb_ref, wo_ref, out_ref):
    merged = (ga_ref[0].astype(F32) * _dot(ya_ref[0], wa_ref[...])
              + gb_ref[0].astype(F32) * _dot(yb_ref[...], wb_ref[...]))
    out_ref[...] = x_ref[...] + _dot(merged.astype(BF16), wo_ref[...])


def _merge(x2, z_all, yb2, wa, wb, wo, tm):
    t, d = x2.shape
    z3 = z_all.reshape(z_all.shape[0], t, d)

    def slab(k):
        return pl.BlockSpec((1, tm, d), lambda i, _k=k: (_k, i, 0))

    row = pl.BlockSpec((tm, d), lambda i: (i, 0))
    wspec = pl.BlockSpec((d, d), lambda i: (0, 0))
    return pl.pallas_call(
        _merge_kernel,
        grid=(t // tm,),
        in_specs=[row, slab(0), row, slab(3), slab(4), wspec, wspec, wspec],
        out_specs=row,
        out_shape=jax.ShapeDtypeStruct((t, d), F32),
        compiler_params=_cparams("arbitrary"),
        name="merge",
    )(x2, z3, yb2, z3, z3, wa, wb, wo)


def _peer_route_kernel(x_ref, g_ref, wq_ref, keys_ref, xn_ref, idx_ref, gw_ref,
                       q_s, sv_s, si_s, cv_s, ce_s, it_s, gt_s):
    tm = x_ref.shape[0]
    k = PEER_TOPK
    big = 1e9

    xn = _rms(x_ref[...], g_ref[...])
    xn_ref[...] = xn
    q_s[...] = _dot(xn.astype(BF16), wq_ref[...]).astype(BF16)

    def sub_topk(hp, carry):
        qh = q_s[:, pl.ds(pl.multiple_of(hp * CHUNK, CHUNK), CHUNK)]
        sc = _dot_nt(keys_ref[hp], qh)
        rowf = lax.broadcasted_iota(I32, sc.shape, 0).astype(F32)
        for j in range(k):
            m = jnp.max(sc, axis=0, keepdims=True)
            i = jnp.min(jnp.where(sc == m, rowf, big), axis=0, keepdims=True)
            sv_s[hp, j:j + 1, :] = m
            si_s[hp, j:j + 1, :] = i
            sc = jnp.where(rowf == i, -jnp.inf, sc)
        return carry
    lax.fori_loop(0, 2 * PEER_HEADS, sub_topk, 0)

    def head_topk(h, carry):
        sv0, sv1 = sv_s[2 * h], sv_s[2 * h + 1]
        si0, si1 = si_s[2 * h], si_s[2 * h + 1]
        cand = jnp.concatenate([sv0[a:a + 1, :] + sv1 for a in range(k)], axis=0)
        eid = jnp.concatenate([si0[a:a + 1, :] * float(CHUNK) + si1 for a in range(k)], axis=0)
        rowf = lax.broadcasted_iota(I32, cand.shape, 0).astype(F32)
        for j in range(k):
            m = jnp.max(cand, axis=0, keepdims=True)
            i = jnp.min(jnp.where(cand == m, rowf, big), axis=0, keepdims=True)
            hit = rowf == i
            cv_s[j:j + 1, :] = m
            ce_s[j:j + 1, :] = jnp.sum(jnp.where(hit, eid, 0.0), axis=0, keepdims=True)
            cand = jnp.where(hit, -jnp.inf, cand)
        cv = cv_s[...]
        e = jnp.exp(cv - cv[0:1, :])
        rows = pl.ds(pl.multiple_of(h * k, k), k)
        gt_s[rows, :] = e / jnp.sum(e, axis=0, keepdims=True)
        it_s[rows, :] = ce_s[...]
        return carry
    lax.fori_loop(0, PEER_HEADS, head_topk, 0)

    idx_ref[...] = it_s[...].T.astype(I32)
    gw_ref[...] = gt_s[...].T


def _peer_route(x2, g, wq, keys, tm):
    t, d = x2.shape
    qd = wq.shape[1]
    nsel = PEER_HEADS * PEER_TOPK
    row = pl.BlockSpec((tm, d), lambda i: (i, 0))
    sel = pl.BlockSpec((tm, nsel), lambda i: (i, 0))
    return pl.pallas_call(
        _peer_route_kernel,
        grid=(t // tm,),
        in_specs=[
            row,
            pl.BlockSpec((1, d), lambda i: (0, 0)),
            pl.BlockSpec((d, qd), lambda i: (0, 0)),
            pl.BlockSpec(keys.shape, lambda i: (0, 0, 0)),
        ],
        out_specs=[row, sel, sel],
        out_shape=[
            jax.ShapeDtypeStruct((t, d), F32),
            jax.ShapeDtypeStruct((t, nsel), I32),
            jax.ShapeDtypeStruct((t, nsel), F32),
        ],
        scratch_shapes=[
            pltpu.VMEM((tm, qd), BF16),
            pltpu.VMEM((2 * PEER_HEADS, PEER_TOPK, tm), F32),
            pltpu.VMEM((2 * PEER_HEADS, PEER_TOPK, tm), F32),
            pltpu.VMEM((PEER_TOPK, tm), F32),
            pltpu.VMEM((PEER_TOPK, tm), F32),
            pltpu.VMEM((nsel, tm), F32),
            pltpu.VMEM((nsel, tm), F32),
        ],
        compiler_params=_cparams("arbitrary"),
        name="peer_route",
    )(x2, g, wq, keys)


SC_LANES = 16
SC_TOK_BLOCK = 8


def _peer_experts(u_tab, v_tab, idx, gw, xn, xres):
    t, d = xn.shape
    k = PEER_TOPK
    nh = PEER_HEADS
    assert k == SC_LANES
    info = plsc.get_sparse_core_info()
    nc, ns = info.num_cores, info.num_subcores
    nw = nc * ns
    tb = SC_TOK_BLOCK
    assert t % (nw * tb) == 0 and d % SC_LANES == 0
    tpw = t // nw
    n_lane_chunks = d // SC_LANES
    mesh = plsc.VectorSubcoreMesh(core_axis_name="c", subcore_axis_name="s")
    idx2 = idx.reshape(t * nh, k)
    gw2 = gw.reshape(t * nh, k)

    @functools.partial(
        pl.kernel,
        mesh=mesh,
        out_type=jax.ShapeDtypeStruct((t, d), F32),
        scratch_types=[
            pltpu.VMEM((tb * nh, k), I32),
            pltpu.VMEM((tb * nh, k), F32),
            pltpu.VMEM((tb, d), F32),
            pltpu.VMEM((tb, d), F32),
            pltpu.VMEM((2, k, d), F32),
            pltpu.VMEM((2, k, d), F32),
            pltpu.VMEM((k, SC_LANES), F32),
            pltpu.VMEM((k,), F32),
            pltpu.SemaphoreType.DMA((2,)),
            pltpu.SemaphoreType.DMA((2,)),
        ],
        compiler_params=pltpu.CompilerParams(needs_layout_passes=False),
        name="peer_experts",
    )
    def run(u_hbm, v_hbm, idx_hbm, gw_hbm, xn_hbm, xres_hbm, out_hbm,
            idx_v, gw_v, xn_v, out_v, ub, vb, red_v, w_v, usem, vsem):
        wid = lax.axis_index("s") * nc + lax.axis_index("c")
        tok0 = wid * tpw
        lane = lax.iota(I32, SC_LANES)

        @pl.loop(0, tpw // tb)
        def _(blk):
            base = tok0 + blk * tb
            pltpu.sync_copy(idx_hbm.at[pl.ds(base * nh, tb * nh)], idx_v)
            pltpu.sync_copy(gw_hbm.at[pl.ds(base * nh, tb * nh)], gw_v)
            pltpu.sync_copy(xn_hbm.at[pl.ds(base, tb)], xn_v)
            pltpu.sync_copy(xres_hbm.at[pl.ds(base, tb)], out_v)

            @pl.loop(0, tb)
            def _(tok):
                def start(h):
                    row = tok * nh + h
                    slot = h % 2
                    cu = pltpu.async_copy(u_hbm.at[idx_v.at[row]], ub.at[slot], usem.at[slot])
                    cv = pltpu.async_copy(v_hbm.at[idx_v.at[row]], vb.at[slot], vsem.at[slot])
                    return cu, cv

                copies = [None] * nh
                copies[0] = start(0)
                for h in range(nh):
                    slot = h % 2
                    if h + 1 < nh:
                        copies[h + 1] = start(h + 1)
                    copies[h][0].wait()

                    def dot_rows(c, accs):
                        off = pl.multiple_of(c * SC_LANES, SC_LANES)
                        xv = xn_v[tok, pl.ds(off, SC_LANES)]
                        return tuple(accs[r] + ub[slot, r, pl.ds(off, SC_LANES)] * xv
                                     for r in range(k))
                    accs = lax.fori_loop(
                        0, n_lane_chunks, dot_rows,
                        tuple(jnp.zeros((SC_LANES,), F32) for _ in range(k)))
                    for r in range(k):
                        red_v[r, :] = accs[r]
                    act = jnp.zeros((SC_LANES,), F32)
                    for jj in range(SC_LANES):
                        act = act + plsc.load_gather(red_v, [lane, jnp.full((SC_LANES,), jj, I32)])
                    z = GELU_C * (act + 0.044715 * act * act * act)
                    tanh_z = 1.0 - 2.0 / (jnp.exp(2.0 * z) + 1.0)
                    w_v[...] = gw_v[tok * nh + h, :] * (0.5 * act * (1.0 + tanh_z))
                    ws = [plsc.load_gather(w_v, [jnp.full((SC_LANES,), r, I32)]) for r in range(k)]

                    copies[h][1].wait()

                    @plsc.parallel_loop(0, n_lane_chunks)
                    def _(c):
                        off = pl.multiple_of(c * SC_LANES, SC_LANES)
                        ps = [vb[slot, r, pl.ds(off, SC_LANES)] * ws[r] for r in range(k)]
                        while len(ps) > 1:
                            ps = [ps[i] + ps[i + 1] for i in range(0, len(ps), 2)]
                        plsc.addupdate(out_v.at[tok, pl.ds(off, SC_LANES)], ps[0])

            pltpu.sync_copy(out_v, out_hbm.at[pl.ds(base, tb)])

    return run(u_tab, v_tab, idx2, gw2, xn, xres)


def _final_norm_kernel(x_ref, g_ref, out_ref):
    out_ref[...] = _rms(x_ref[...], g_ref[...])


def _final_norm(x2, g, tm):
    t, d = x2.shape
    row = pl.BlockSpec((tm, d), lambda i: (i, 0))
    return pl.pallas_call(
        _final_norm_kernel,
        grid=(t // tm,),
        in_specs=[row, pl.BlockSpec((1, d), lambda i: (0, 0))],
        out_specs=row,
        out_shape=jax.ShapeDtypeStruct((t, d), F32),
        compiler_params=_cparams("arbitrary"),
        name="final_norm",
    )(x2, g)


def _row_tile(t, want):
    tm = min(want, t)
    assert t % tm == 0
    return tm


def kernel(x, norm_mix_g, w_in, gm_norm_g, gm_w_s, gm_b_s, ml_conv_w, ml_conv_b, ml_w_q, ml_w_k, ml_w_v, ml_w_gate, ml_b_gate, ml_head_g, ml_skip, w_branch_a, w_branch_b, w_out, norm_ffn_g, peer_w_query, peer_sub_keys, peer_u, peer_v, final_g):
    b, s, d = x.shape
    depth = w_in.shape[0]
    assert s % CHUNK == 0 and d % (ML_HEADS * CHUNK) == 0 and w_in.shape[2] == N_IN_SEG * d
    n_groups = BATCH_GROUPS if b % BATCH_GROUPS == 0 else 1
    bg = b // n_groups
    t = bg * s
    tm_merge = _row_tile(t, 512)
    tm_route = _row_tile(t, 256)
    parts = [x[i * bg:(i + 1) * bg].reshape(t, d) for i in range(n_groups)]
    for l in range(depth):
        w_in_b = w_in[l].astype(BF16)
        w_s_b = gm_w_s[l].astype(BF16)
        wq_b, wk_b, wv_b = (ml_w_q[l].astype(BF16), ml_w_k[l].astype(BF16), ml_w_v[l].astype(BF16))
        wg = ml_w_gate[l].astype(BF16)
        wa_b, wb_b, wo_b = (w_branch_a[l].astype(BF16), w_branch_b[l].astype(BF16),
                            w_out[l].astype(BF16))
        wquery_b = peer_w_query[l].astype(BF16)
        keys = peer_sub_keys[l].reshape(2 * PEER_HEADS, CHUNK, -1).astype(BF16)
        for p in range(n_groups):
            x2 = parts[p]
            z_all = _in_proj(
                x2.reshape(bg, s, d), norm_mix_g[l].reshape(1, d), w_in_b,
                gm_norm_g[l].reshape(1, d), w_s_b, gm_b_s[l].T)
            yb = _mlstm(
                z_all, ml_conv_w[l], ml_conv_b[l].reshape(1, d), wq_b, wk_b, wv_b,
                wg, wg.T, ml_b_gate[l].reshape(1, -1), ml_b_gate[l].reshape(-1, 1),
                ml_head_g[l].reshape(1, d), ml_skip[l].reshape(1, d))
            x2 = _merge(x2, z_all, yb.reshape(t, d), wa_b, wb_b, wo_b, tm_merge)
            xn, idx, gw = _peer_route(x2, norm_ffn_g[l].reshape(1, d), wquery_b, keys, tm_route)
            parts[p] = _peer_experts(peer_u[l], peer_v[l], idx, gw, xn, x2)
    outs = [_final_norm(p, final_g.reshape(1, d), tm_merge) for p in parts]
    return jnp.concatenate(outs, axis=0).reshape(b, s, d)
```

```python
import functools

import jax
import jax.numpy as jnp
from jax import lax
from jax.experimental import pallas as pl
from jax.experimental.pallas import tpu as pltpu
from jax.experimental.pallas import tpu_sc as plsc

F32 = jnp.float32
BF16 = jnp.bfloat16
I32 = jnp.int32

EPS = 1e-6
NEG = -1e30
CHUNK = 128
GM_GROUPS = 8
ML_HEADS = 4
ML_CONV = 5
CONV_PAD = 8
PEER_HEADS = 8
PEER_TOPK = 16
N_IN_SEG = 6
BATCH_GROUPS = 2
GELU_C = 0.7978845608028654

VMEM_LIMIT = 56 * 1024 * 1024


def _cparams(*sem):
    return pltpu.CompilerParams(dimension_semantics=sem, vmem_limit_bytes=VMEM_LIMIT)


def _rows(c, n=CHUNK):
    return pl.ds(pl.multiple_of(c * n, n), n)


def _rms(x, g):
    return x * lax.rsqrt(jnp.mean(x * x, axis=-1, keepdims=True) + EPS) * g


def _dot(a, b):
    return jnp.dot(a, b, preferred_element_type=F32)


def _dot_nt(a, b):
    return lax.dot_general(a, b, (((1,), (1,)), ((), ())), preferred_element_type=F32)


def _dot_tn(a, b):
    return lax.dot_general(a, b, (((0,), (0,)), ((), ())), preferred_element_type=F32)


def _in_proj_kernel(x_ref, g_ref, w_ref, gmg_ref, ws_ref, bst_ref, out_ref, hn_s, gu_s):
    j = pl.program_id(1)
    n_chunks = x_ref.shape[1] // CHUNK
    d = x_ref.shape[2]

    def z_of(c):
        return _dot(hn_s[_rows(c), :], w_ref[...])

    def for_chunks(fn):
        def body(c, carry):
            fn(c)
            return carry
        lax.fori_loop(0, n_chunks, body, 0)

    @pl.when(j == 0)
    def _():
        def norm(c):
            hn_s[_rows(c), :] = _rms(x_ref[0, _rows(c), :], g_ref[...]).astype(BF16)
        for_chunks(norm)

        def seg_u(c):
            gu_s[_rows(c), :] = jax.nn.gelu(z_of(c)).astype(BF16)
        for_chunks(seg_u)

    @pl.when(j == 1)
    def _():
        def seg_v(c):
            vn = _rms(jax.nn.gelu(z_of(c)), gmg_ref[...]).astype(BF16)
            gu = gu_s[_rows(c), :]
            for g in range(GM_GROUPS):
                cs = slice(g * (d // GM_GROUPS), (g + 1) * (d // GM_GROUPS))
                mixed = _dot(ws_ref[g], vn[:, cs]) + bst_ref[:, g:g + 1]
                out_ref[0, 0, _rows(c), cs] = (gu[:, cs].astype(F32) * mixed).astype(BF16)
        for_chunks(seg_v)

    @pl.when(j == 2)
    def _():
        def seg_xm(c):
            out_ref[0, 0, _rows(c), :] = z_of(c).astype(BF16)
        for_chunks(seg_xm)

    @pl.when(j >= 3)
    def _():
        def seg_gate(c):
            out_ref[0, 0, _rows(c), :] = jax.nn.sigmoid(z_of(c)).astype(BF16)
        for_chunks(seg_gate)


def _in_proj(x3, g, w_in, gm_g, w_s, b_st):
    b, s, d = x3.shape
    return pl.pallas_call(
        _in_proj_kernel,
        grid=(b, N_IN_SEG),
        in_specs=[
            pl.BlockSpec((1, s, d), lambda i, j: (i, 0, 0)),
            pl.BlockSpec((1, d), lambda i, j: (0, 0)),
            pl.BlockSpec((d, d), lambda i, j: (0, j)),
            pl.BlockSpec((1, d), lambda i, j: (0, 0)),
            pl.BlockSpec((GM_GROUPS, CHUNK, CHUNK), lambda i, j: (0, 0, 0)),
            pl.BlockSpec((CHUNK, GM_GROUPS), lambda i, j: (0, 0)),
        ],
        out_specs=pl.BlockSpec((1, 1, s, d), lambda i, j: (jnp.maximum(j - 1, 0), i, 0, 0)),
        out_shape=jax.ShapeDtypeStruct((N_IN_SEG - 1, b, s, d), BF16),
        scratch_shapes=[pltpu.VMEM((s, d), BF16), pltpu.VMEM((s, d), BF16)],
        compiler_params=_cparams("arbitrary", "arbitrary"),
        name="in_proj",
    )(x3, g, w_in, gm_g, w_s, b_st)


def _split3(a):
    hi = a.astype(BF16)
    r1 = a - hi.astype(F32)
    mid = r1.astype(BF16)
    lo = (r1 - mid.astype(F32)).astype(BF16)
    return hi, mid, lo


def _log_sigmoid(x):
    return jnp.minimum(x, 0.0) - jnp.log(1.0 + jnp.exp(-jnp.abs(x)))


def _mlstm_kernel(xm_ref, so_ref, cw_ref, cb_ref, wq_ref, wk_ref, wv_ref, wg_ref, wgt_ref,
                  bg_ref, bgt_ref, hg_ref, sk_ref, out_ref,
                  pad_s, xc_s, gc_s, gr_s, q_s, k_s, v_s, hf_s, ct_s):
    s = xm_ref.shape[2]
    d = xm_ref.shape[3]
    dh = d // ML_HEADS
    n_chunks = s // CHUNK
    nh = ML_HEADS

    ri = lax.broadcasted_iota(I32, (CHUNK, CHUNK), 0)
    ci = lax.broadcasted_iota(I32, (CHUNK, CHUNK), 1)
    tril = ri >= ci
    triu = ri <= ci
    tril_b = jnp.where(tril, 1.0, 0.0).astype(BF16)
    triu_b = jnp.where(triu, 1.0, 0.0).astype(BF16)

    pad_s[0:CONV_PAD, :] = jnp.zeros((CONV_PAD, d), F32)
    pad_s[s + CONV_PAD:s + 2 * CONV_PAD, :] = jnp.zeros((CONV_PAD, d), F32)

    def fill(c, carry):
        pad_s[pl.ds(pl.multiple_of(c * CHUNK, CHUNK) + CONV_PAD, CHUNK), :] = (
            xm_ref[0, 0, _rows(c), :].astype(F32))
        return carry
    lax.fori_loop(0, n_chunks, fill, 0)

    lane = lax.broadcasted_iota(I32, (s, CHUNK), 1)
    v_s[:, dh:dh + CHUNK] = jnp.where(lane == 0, 1.0, 0.0).astype(BF16)

    def conv_gates(c, carry):
        base = pl.multiple_of(c * CHUNK, CHUNK)
        gc = jnp.zeros((CHUNK, 4 * nh), F32) + bg_ref[...]
        gr = jnp.zeros((4 * nh, CHUNK), F32) + bgt_ref[...]
        for hh in range(nh):
            cs = slice(hh * dh, (hh + 1) * dh)
            acc = jnp.zeros((CHUNK, dh), F32) + cb_ref[:, cs]
            win = pad_s[pl.ds(base, CHUNK + 2 * CONV_PAD), cs]
            for t in range(ML_CONV):
                off = CONV_PAD - ML_CONV // 2 + t
                acc = acc + win[off:off + CHUNK, :] * cw_ref[t:t + 1, cs]
            xc = (acc * jax.nn.sigmoid(acc)).astype(BF16)
            xc_s[_rows(c), cs] = xc
            gc = gc + _dot(xc, wg_ref[cs, :])
            gr = gr + _dot_nt(wgt_ref[:, cs], xc)
        lf_c = _log_sigmoid(gc[:, nh:2 * nh])
        lb_c = _log_sigmoid(gc[:, 3 * nh:4 * nh])
        lf_r = _log_sigmoid(gr[nh:2 * nh, :])
        lb_r = _log_sigmoid(gr[3 * nh:4 * nh, :])
        bcf = sum(_dot(tril_b, p) for p in _split3(lf_c))
        bcb = sum(_dot(triu_b, p) for p in _split3(lb_c))
        brf = sum(_dot(p, triu_b) for p in _split3(lf_r))
        brb = sum(_dot(p, tril_b) for p in _split3(lb_r))
        gc_s[_rows(c), 0:nh] = gc[:, 0:nh]
        gc_s[_rows(c), nh:2 * nh] = bcf
        gc_s[_rows(c), 2 * nh:3 * nh] = gc[:, 2 * nh:3 * nh]
        gc_s[_rows(c), 3 * nh:4 * nh] = bcb
        gr_s[0:nh, _rows(c)] = gr[0:nh, :]
        gr_s[nh:2 * nh, _rows(c)] = brf
        gr_s[2 * nh:3 * nh, _rows(c)] = gr[2 * nh:3 * nh, :]
        gr_s[3 * nh:4 * nh, _rows(c)] = brb
        return carry
    lax.fori_loop(0, n_chunks, conv_gates, 0)

    for hh in range(nh):
        cs = slice(hh * dh, (hh + 1) * dh)

        def qkv(c, carry):
            xc = xc_s[_rows(c), cs]
            q_s[_rows(c), :] = _dot(xc, wq_ref[hh]).astype(BF16)
            k_s[_rows(c), :] = (_dot(xc, wk_ref[hh]) * (dh ** -0.5)).astype(BF16)
            v_s[_rows(c), 0:dh] = _dot(xm_ref[0, 0, _rows(c), cs], wv_ref[hh]).astype(BF16)
            return carry
        lax.fori_loop(0, n_chunks, qkv, 0)

        def step(c, m, i_col, b_col, mask, last_row):
            r = _rows(c)
            qc, kc, vx = q_s[r, :], k_s[r, :], v_s[r, :]
            bc = gc_s[r, b_col:b_col + 1]
            ic = gc_s[r, i_col:i_col + 1]
            br = gr_s[b_col:b_col + 1, r]
            ir = gr_s[i_col:i_col + 1, r]
            dm = jnp.where(mask, bc - br + ir, NEG)
            inter = bc + m
            m_t = jnp.maximum(inter, jnp.max(dm, axis=-1, keepdims=True))
            w_intra = jnp.exp(dm - m_t)
            w_inter = jnp.exp(inter - m_t)
            sc = (_dot_nt(qc, kc) * w_intra).astype(BF16)
            res = _dot(sc, vx) + w_inter * _dot(qc, ct_s[...].astype(BF16))
            num = res[:, 0:dh]
            den = res[:, dh:dh + 1]
            h = num / jnp.maximum(jnp.abs(den), jnp.exp(-m_t))
            b_l = bc[last_row:last_row + 1, :]
            g_col = b_l - bc + ic
            m_new = jnp.maximum(b_l + m, jnp.max(g_col, axis=0, keepdims=True))
            w_s = jnp.exp(g_col - m_new)
            w_c = jnp.exp(b_l + m - m_new)
            ct_s[...] = w_c * ct_s[...] + _dot_tn(kc, (w_s * vx.astype(F32)).astype(BF16))
            return h, m_new

        ct_s[...] = jnp.zeros(ct_s.shape, F32)

        def fwd(c, m):
            h, m_new = step(c, m, hh, nh + hh, tril, CHUNK - 1)
            hf_s[_rows(c), :] = h
            return m_new
        lax.fori_loop(0, n_chunks, fwd, jnp.zeros((1, 1), F32))

        ct_s[...] = jnp.zeros(ct_s.shape, F32)

        def bwd(i, m):
            c = n_chunks - 1 - i
            h, m_new = step(c, m, 2 * nh + hh, 3 * nh + hh, triu, 0)
            h = h + hf_s[_rows(c), :]
            y = _rms(h, hg_ref[:, cs])
            xc = xc_s[_rows(c), cs].astype(F32)
            so = so_ref[0, 0, _rows(c), cs].astype(F32)
            out_ref[0, _rows(c), cs] = (so * (y + sk_ref[:, cs] * xc)).astype(BF16)
            return m_new
        lax.fori_loop(0, n_chunks, bwd, jnp.zeros((1, 1), F32))


def _mlstm(z_all, cw, cb, wq, wk, wv, wg, wgt, bg, bgt, hg, sk):
    _, b, s, d = z_all.shape
    dh = d // ML_HEADS
    ng = 4 * ML_HEADS

    def full(a):
        return pl.BlockSpec(a.shape, lambda i, _n=a.ndim: (0,) * _n)

    return pl.pallas_call(
        _mlstm_kernel,
        grid=(b,),
        in_specs=[
            pl.BlockSpec((1, 1, s, d), lambda i: (1, i, 0, 0)),
            pl.BlockSpec((1, 1, s, d), lambda i: (2, i, 0, 0)),
            full(cw), full(cb), full(wq), full(wk), full(wv), full(wg), full(wgt),
            full(bg), full(bgt), full(hg), full(sk),
        ],
        out_specs=pl.BlockSpec((1, s, d), lambda i: (i, 0, 0)),
        out_shape=jax.ShapeDtypeStruct((b, s, d), BF16),
        scratch_shapes=[
            pltpu.VMEM((s + 2 * CONV_PAD, d), F32),
            pltpu.VMEM((s, d), BF16),
            pltpu.VMEM((s, ng), F32),
            pltpu.VMEM((ng, s), F32),
            pltpu.VMEM((s, dh), BF16),
            pltpu.VMEM((s, dh), BF16),
            pltpu.VMEM((s, dh + CHUNK), BF16),
            pltpu.VMEM((s, dh), F32),
            pltpu.VMEM((dh, dh + CHUNK), F32),
        ],
        compiler_params=_cparams("arbitrary"),
        name="mlstm",
    )(z_all, z_all, cw, cb, wq, wk, wv, wg, wgt, bg, bgt, hg, sk)


def _merge_kernel(x_ref, ya_ref, yb_ref, ga_ref, gb_ref, wa_ref, wb_ref, wo_ref, out_ref):
    merged = (ga_ref[0].astype(F32) * _dot(ya_ref[0], wa_ref[...])
              + gb_ref[0].astype(F32) * _dot(yb_ref[...], wb_ref[...]))
    out_ref[...] = x_ref[...] + _dot(merged.astype(BF16), wo_ref[...])


def _merge(x2, z_all, yb2, wa, wb, wo, tm):
    t, d = x2.shape
    z3 = z_all.reshape(z_all.shape[0], t, d)

    def slab(k):
        return pl.BlockSpec((1, tm, d), lambda i, _k=k: (_k, i, 0))

    row = pl.BlockSpec((tm, d), lambda i: (i, 0))
    wspec = pl.BlockSpec((d, d), lambda i: (0, 0))
    return pl.pallas_call(
        _merge_kernel,
        grid=(t // tm,),
        in_specs=[row, slab(0), row, slab(3), slab(4), wspec, wspec, wspec],
        out_specs=row,
        out_shape=jax.ShapeDtypeStruct((t, d), F32),
        compiler_params=_cparams("arbitrary"),
        name="merge",
    )(x2, z3, yb2, z3, z3, wa, wb, wo)


def _peer_route_kernel(x_ref, g_ref, wq_ref, keys_ref, xn_ref, idx_ref, gw_ref,
                       q_s, sv_s, si_s, cv_s, ce_s, it_s, gt_s):
    tm = x_ref.shape[0]
    k = PEER_TOPK
    big = 1e9

    xn = _rms(x_ref[...], g_ref[...])
    xn_ref[...] = xn
    q_s[...] = _dot(xn.astype(BF16), wq_ref[...]).astype(BF16)

    def sub_topk(hp, carry):
        qh = q_s[:, pl.ds(pl.multiple_of(hp * CHUNK, CHUNK), CHUNK)]
        sc = _dot_nt(keys_ref[hp], qh)
        rowf = lax.broadcasted_iota(I32, sc.shape, 0).astype(F32)
        for j in range(k):
            m = jnp.max(sc, axis=0, keepdims=True)
            i = jnp.min(jnp.where(sc == m, rowf, big), axis=0, keepdims=True)
            sv_s[hp, j:j + 1, :] = m
            si_s[hp, j:j + 1, :] = i
            sc = jnp.where(rowf == i, -jnp.inf, sc)
        return carry
    lax.fori_loop(0, 2 * PEER_HEADS, sub_topk, 0)

    def head_topk(h, carry):
        sv0, sv1 = sv_s[2 * h], sv_s[2 * h + 1]
        si0, si1 = si_s[2 * h], si_s[2 * h + 1]
        cand = jnp.concatenate([sv0[a:a + 1, :] + sv1 for a in range(k)], axis=0)
        eid = jnp.concatenate([si0[a:a + 1, :] * float(CHUNK) + si1 for a in range(k)], axis=0)
        rowf = lax.broadcasted_iota(I32, cand.shape, 0).astype(F32)
        for j in range(k):
            m = jnp.max(cand, axis=0, keepdims=True)
            i = jnp.min(jnp.where(cand == m, rowf, big), axis=0, keepdims=True)
            hit = rowf == i
            cv_s[j:j + 1, :] = m
            ce_s[j:j + 1, :] = jnp.sum(jnp.where(hit, eid, 0.0), axis=0, keepdims=True)
            cand = jnp.where(hit, -jnp.inf, cand)
        cv = cv_s[...]
        e = jnp.exp(cv - cv[0:1, :])
        rows = pl.ds(pl.multiple_of(h * k, k), k)
        gt_s[rows, :] = e / jnp.sum(e, axis=0, keepdims=True)
        it_s[rows, :] = ce_s[...]
        return carry
    lax.fori_loop(0, PEER_HEADS, head_topk, 0)

    idx_ref[...] = it_s[...].T.astype(I32)
    gw_ref[...] = gt_s[...].T


def _peer_route(x2, g, wq, keys, tm):
    t, d = x2.shape
    qd = wq.shape[1]
    nsel = PEER_HEADS * PEER_TOPK
    row = pl.BlockSpec((tm, d), lambda i: (i, 0))
    sel = pl.BlockSpec((tm, nsel), lambda i: (i, 0))
    return pl.pallas_call(
        _peer_route_kernel,
        grid=(t // tm,),
        in_specs=[
            row,
            pl.BlockSpec((1, d), lambda i: (0, 0)),
            pl.BlockSpec((d, qd), lambda i: (0, 0)),
            pl.BlockSpec(keys.shape, lambda i: (0, 0, 0)),
        ],
        out_specs=[row, sel, sel],
        out_shape=[
            jax.ShapeDtypeStruct((t, d), F32),
            jax.ShapeDtypeStruct((t, nsel), I32),
            jax.ShapeDtypeStruct((t, nsel), F32),
        ],
        scratch_shapes=[
            pltpu.VMEM((tm, qd), BF16),
            pltpu.VMEM((2 * PEER_HEADS, PEER_TOPK, tm), F32),
            pltpu.VMEM((2 * PEER_HEADS, PEER_TOPK, tm), F32),
            pltpu.VMEM((PEER_TOPK, tm), F32),
            pltpu.VMEM((PEER_TOPK, tm), F32),
            pltpu.VMEM((nsel, tm), F32),
            pltpu.VMEM((nsel, tm), F32),
        ],
        compiler_params=_cparams("arbitrary"),
        name="peer_route",
    )(x2, g, wq, keys)


SC_LANES = 16
SC_TOK_BLOCK = 8
SC_SLOTS = 4
SC_AHEAD = 3
PACK = 2 * SC_LANES


def _pack_rows(tab):
    e, d = tab.shape
    bits = lax.bitcast_convert_type(tab.astype(BF16), jnp.uint16).astype(jnp.uint32)
    bits = bits.reshape(e, d // PACK, 2, SC_LANES)
    words = bits[:, :, 0, :] | (bits[:, :, 1, :] << 16)
    return lax.bitcast_convert_type(words.reshape(e, d // 2), I32)


def _unpack_words(w):
    lo = lax.bitcast_convert_type(lax.shift_left(w, jnp.full(w.shape, 16, I32)), F32)
    hi = lax.bitcast_convert_type(w & jnp.full(w.shape, -65536, I32), F32)
    return lo, hi


def _peer_experts(u_pk, v_pk, idx, gw, xn, xres):
    t, d = xn.shape
    k = PEER_TOPK
    nh = PEER_HEADS
    dw = d // 2
    assert k == SC_LANES and nh % SC_SLOTS == 0 and SC_AHEAD < SC_SLOTS and d % PACK == 0
    info = plsc.get_sparse_core_info()
    nc, ns = info.num_cores, info.num_subcores
    nw = nc * ns
    tb = SC_TOK_BLOCK
    assert t % (nw * tb) == 0
    tpw = t // nw
    n_groups = d // PACK
    mesh = plsc.VectorSubcoreMesh(core_axis_name="c", subcore_axis_name="s")
    idx2 = idx.reshape(t * nh, k)
    gw2 = gw.reshape(t * nh, k)

    @functools.partial(
        pl.kernel,
        mesh=mesh,
        out_type=jax.ShapeDtypeStruct((t, d), F32),
        scratch_types=[
            pltpu.VMEM((tb * nh, k), I32),
            pltpu.VMEM((tb * nh, k), F32),
            pltpu.VMEM((tb, d), F32),
            pltpu.VMEM((tb, d), F32),
            pltpu.VMEM((SC_SLOTS, k, dw), I32),
            pltpu.VMEM((SC_SLOTS, k, dw), I32),
            pltpu.VMEM((k, SC_LANES), F32),
            pltpu.VMEM((k,), F32),
            pltpu.SemaphoreType.DMA((SC_SLOTS,)),
            pltpu.SemaphoreType.DMA((SC_SLOTS,)),
        ],
        compiler_params=pltpu.CompilerParams(needs_layout_passes=False),
        name="peer_experts",
    )
    def run(u_hbm, v_hbm, idx_hbm, gw_hbm, xn_hbm, xres_hbm, out_hbm,
            idx_v, gw_v, xn_v, out_v, ub, vb, red_v, w_v, usem, vsem):
        wid = lax.axis_index("s") * nc + lax.axis_index("c")
        tok0 = wid * tpw
        lane = lax.iota(I32, SC_LANES)

        def gathers(tok, h):
            row = tok * nh + h
            slot = h % SC_SLOTS
            return (pltpu.make_async_copy(u_hbm.at[idx_v.at[row]], ub.at[slot], usem.at[slot]),
                    pltpu.make_async_copy(v_hbm.at[idx_v.at[row]], vb.at[slot], vsem.at[slot]))

        def start(tok, h):
            cu, cv = gathers(tok, h)
            cu.start()
            cv.start()

        @pl.loop(0, tpw // tb)
        def _(blk):
            base = tok0 + blk * tb
            pltpu.sync_copy(idx_hbm.at[pl.ds(base * nh, tb * nh)], idx_v)
            pltpu.sync_copy(gw_hbm.at[pl.ds(base * nh, tb * nh)], gw_v)
            pltpu.sync_copy(xn_hbm.at[pl.ds(base, tb)], xn_v)
            pltpu.sync_copy(xres_hbm.at[pl.ds(base, tb)], out_v)
            for h in range(SC_AHEAD):
                start(0, h)

            @pl.loop(0, tb)
            def _(tok):
                for h in range(nh):
                    slot = h % SC_SLOTS
                    ahead = h + SC_AHEAD
                    if ahead < nh:
                        start(tok, ahead)
                    else:
                        @pl.when(tok + 1 < tb)
                        def _():
                            start(tok + 1, ahead - nh)
                    cu, cv = gathers(tok, h)
                    cu.wait()

                    def dot_rows(c, accs):
                        xo = pl.multiple_of(c * PACK, PACK)
                        wo = pl.multiple_of(c * SC_LANES, SC_LANES)
                        x_lo = xn_v[tok, pl.ds(xo, SC_LANES)]
                        x_hi = xn_v[tok, pl.ds(xo + SC_LANES, SC_LANES)]
                        out = []
                        for r in range(k):
                            lo, hi = _unpack_words(ub[slot, r, pl.ds(wo, SC_LANES)])
                            out.append(accs[r] + lo * x_lo + hi * x_hi)
                        return tuple(out)
                    accs = lax.fori_loop(
                        0, n_groups, dot_rows,
                        tuple(jnp.zeros((SC_LANES,), F32) for _ in range(k)))
                    for r in range(k):
                        red_v[r, :] = accs[r]
                    act = jnp.zeros((SC_LANES,), F32)
                    for jj in range(SC_LANES):
                        act = act + plsc.load_gather(red_v, [lane, jnp.full((SC_LANES,), jj, I32)])
                    z = GELU_C * (act + 0.044715 * act * act * act)
                    tanh_z = 1.0 - 2.0 / (jnp.exp(2.0 * z) + 1.0)
                    w_v[...] = gw_v[tok * nh + h, :] * (0.5 * act * (1.0 + tanh_z))
                    ws = [plsc.load_gather(w_v, [jnp.full((SC_LANES,), r, I32)]) for r in range(k)]

                    cv.wait()

                    @plsc.parallel_loop(0, n_groups)
                    def _(c):
                        xo = pl.multiple_of(c * PACK, PACK)
                        wo = pl.multiple_of(c * SC_LANES, SC_LANES)
                        los, his = [], []
                        for r in range(k):
                            lo, hi = _unpack_words(vb[slot, r, pl.ds(wo, SC_LANES)])
                            los.append(lo * ws[r])
                            his.append(hi * ws[r])
                        while len(los) > 1:
                            los = [los[i] + los[i + 1] for i in range(0, len(los), 2)]
                            his = [his[i] + his[i + 1] for i in range(0, len(his), 2)]
                        plsc.addupdate(out_v.at[tok, pl.ds(xo, SC_LANES)], los[0])
                        plsc.addupdate(out_v.at[tok, pl.ds(xo + SC_LANES, SC_LANES)], his[0])

            pltpu.sync_copy(out_v, out_hbm.at[pl.ds(base, tb)])

    return run(u_pk, v_pk, idx2, gw2, xn, xres)


def _final_norm_kernel(x_ref, g_ref, out_ref):
    out_ref[...] = _rms(x_ref[...], g_ref[...])


def _final_norm(x2, g, tm):
    t, d = x2.shape
    row = pl.BlockSpec((tm, d), lambda i: (i, 0))
    return pl.pallas_call(
        _final_norm_kernel,
        grid=(t // tm,),
        in_specs=[row, pl.BlockSpec((1, d), lambda i: (0, 0))],
        out_specs=row,
        out_shape=jax.ShapeDtypeStruct((t, d), F32),
        compiler_params=_cparams("arbitrary"),
        name="final_norm",
    )(x2, g)


def _row_tile(t, want):
    tm = min(want, t)
    assert t % tm == 0
    return tm


def kernel(x, norm_mix_g, w_in, gm_norm_g, gm_w_s, gm_b_s, ml_conv_w, ml_conv_b, ml_w_q, ml_w_k, ml_w_v, ml_w_gate, ml_b_gate, ml_head_g, ml_skip, w_branch_a, w_branch_b, w_out, norm_ffn_g, peer_w_query, peer_sub_keys, peer_u, peer_v, final_g):
    b, s, d = x.shape
    depth = w_in.shape[0]
    assert s % CHUNK == 0 and d % (ML_HEADS * CHUNK) == 0 and w_in.shape[2] == N_IN_SEG * d
    n_groups = BATCH_GROUPS if b % BATCH_GROUPS == 0 else 1
    bg = b // n_groups
    t = bg * s
    tm_merge = _row_tile(t, 512)
    tm_route = _row_tile(t, 256)
    parts = [x[i * bg:(i + 1) * bg].reshape(t, d) for i in range(n_groups)]
    for l in range(depth):
        w_in_b = w_in[l].astype(BF16)
        w_s_b = gm_w_s[l].astype(BF16)
        wq_b, wk_b, wv_b = (ml_w_q[l].astype(BF16), ml_w_k[l].astype(BF16), ml_w_v[l].astype(BF16))
        wg = ml_w_gate[l].astype(BF16)
        wa_b, wb_b, wo_b = (w_branch_a[l].astype(BF16), w_branch_b[l].astype(BF16),
                            w_out[l].astype(BF16))
        wquery_b = peer_w_query[l].astype(BF16)
        keys = peer_sub_keys[l].reshape(2 * PEER_HEADS, CHUNK, -1).astype(BF16)
        u_pk, v_pk = _pack_rows(peer_u[l]), _pack_rows(peer_v[l])
        for p in range(n_groups):
            x2 = parts[p]
            z_all = _in_proj(
                x2.reshape(bg, s, d), norm_mix_g[l].reshape(1, d), w_in_b,
                gm_norm_g[l].reshape(1, d), w_s_b, gm_b_s[l].T)
            yb = _mlstm(
                z_all, ml_conv_w[l], ml_conv_b[l].reshape(1, d), wq_b, wk_b, wv_b,
                wg, wg.T, ml_b_gate[l].reshape(1, -1), ml_b_gate[l].reshape(-1, 1),
                ml_head_g[l].reshape(1, d), ml_skip[l].reshape(1, d))
            x2 = _merge(x2, z_all, yb.reshape(t, d), wa_b, wb_b, wo_b, tm_merge)
            xn, idx, gw = _peer_route(x2, norm_ffn_g[l].reshape(1, d), wquery_b, keys, tm_route)
            parts[p] = _peer_experts(u_pk, v_pk, idx, gw, xn, x2)
    outs = [_final_norm(p, final_g.reshape(1, d), tm_merge) for p in parts]
    return jnp.concatenate(outs, axis=0).reshape(b, s, d)
```

```python
import functools

import jax
import jax.numpy as jnp
from jax import lax
from jax.experimental import pallas as pl
from jax.experimental.pallas import tpu as pltpu
from jax.experimental.pallas import tpu_sc as plsc

F32 = jnp.float32
BF16 = jnp.bfloat16
I32 = jnp.int32

EPS = 1e-6
NEG = -1e30
CHUNK = 128
GM_GROUPS = 8
ML_HEADS = 4
ML_CONV = 5
CONV_PAD = 8
PEER_HEADS = 8
PEER_TOPK = 16
N_IN_SEG = 6
BATCH_GROUPS = 4
GELU_C = 0.7978845608028654

VMEM_LIMIT = 56 * 1024 * 1024


def _cparams(*sem):
    return pltpu.CompilerParams(dimension_semantics=sem, vmem_limit_bytes=VMEM_LIMIT)


def _rows(c, n=CHUNK):
    return pl.ds(pl.multiple_of(c * n, n), n)


def _rms(x, g):
    return x * lax.rsqrt(jnp.mean(x * x, axis=-1, keepdims=True) + EPS) * g


def _dot(a, b):
    return jnp.dot(a, b, preferred_element_type=F32)


def _dot_nt(a, b):
    return lax.dot_general(a, b, (((1,), (1,)), ((), ())), preferred_element_type=F32)


def _dot_tn(a, b):
    return lax.dot_general(a, b, (((0,), (0,)), ((), ())), preferred_element_type=F32)


def _in_proj_kernel(x_ref, g_ref, w_ref, gmg_ref, ws_ref, bst_ref, out_ref, hn_s, gu_s):
    j = pl.program_id(1)
    n_chunks = x_ref.shape[1] // CHUNK
    d = x_ref.shape[2]

    def z_of(c):
        return _dot(hn_s[_rows(c), :], w_ref[...])

    def for_chunks(fn):
        def body(c, carry):
            fn(c)
            return carry
        lax.fori_loop(0, n_chunks, body, 0)

    @pl.when(j == 0)
    def _():
        def norm(c):
            hn_s[_rows(c), :] = _rms(x_ref[0, _rows(c), :], g_ref[...]).astype(BF16)
        for_chunks(norm)

        def seg_u(c):
            gu_s[_rows(c), :] = jax.nn.gelu(z_of(c)).astype(BF16)
        for_chunks(seg_u)

    @pl.when(j == 1)
    def _():
        def seg_v(c):
            vn = _rms(jax.nn.gelu(z_of(c)), gmg_ref[...]).astype(BF16)
            gu = gu_s[_rows(c), :]
            for g in range(GM_GROUPS):
                cs = slice(g * (d // GM_GROUPS), (g + 1) * (d // GM_GROUPS))
                mixed = _dot(ws_ref[g], vn[:, cs]) + bst_ref[:, g:g + 1]
                out_ref[0, 0, _rows(c), cs] = (gu[:, cs].astype(F32) * mixed).astype(BF16)
        for_chunks(seg_v)

    @pl.when(j == 2)
    def _():
        def seg_xm(c):
            out_ref[0, 0, _rows(c), :] = z_of(c).astype(BF16)
        for_chunks(seg_xm)

    @pl.when(j >= 3)
    def _():
        def seg_gate(c):
            out_ref[0, 0, _rows(c), :] = jax.nn.sigmoid(z_of(c)).astype(BF16)
        for_chunks(seg_gate)


def _in_proj(x3, g, w_in, gm_g, w_s, b_st):
    b, s, d = x3.shape
    return pl.pallas_call(
        _in_proj_kernel,
        grid=(b, N_IN_SEG),
        in_specs=[
            pl.BlockSpec((1, s, d), lambda i, j: (i, 0, 0)),
            pl.BlockSpec((1, d), lambda i, j: (0, 0)),
            pl.BlockSpec((d, d), lambda i, j: (0, j)),
            pl.BlockSpec((1, d), lambda i, j: (0, 0)),
            pl.BlockSpec((GM_GROUPS, CHUNK, CHUNK), lambda i, j: (0, 0, 0)),
            pl.BlockSpec((CHUNK, GM_GROUPS), lambda i, j: (0, 0)),
        ],
        out_specs=pl.BlockSpec((1, 1, s, d), lambda i, j: (jnp.maximum(j - 1, 0), i, 0, 0)),
        out_shape=jax.ShapeDtypeStruct((N_IN_SEG - 1, b, s, d), BF16),
        scratch_shapes=[pltpu.VMEM((s, d), BF16), pltpu.VMEM((s, d), BF16)],
        compiler_params=_cparams("arbitrary", "arbitrary"),
        name="in_proj",
    )(x3, g, w_in, gm_g, w_s, b_st)


def _split3(a):
    hi = a.astype(BF16)
    r1 = a - hi.astype(F32)
    mid = r1.astype(BF16)
    lo = (r1 - mid.astype(F32)).astype(BF16)
    return hi, mid, lo


def _log_sigmoid(x):
    return jnp.minimum(x, 0.0) - jnp.log(1.0 + jnp.exp(-jnp.abs(x)))


def _mlstm_kernel(xm_ref, so_ref, cw_ref, cb_ref, wq_ref, wk_ref, wv_ref, wg_ref, wgt_ref,
                  bg_ref, bgt_ref, hg_ref, sk_ref, out_ref,
                  pad_s, xc_s, gc_s, gr_s, q_s, k_s, v_s, hf_s, ct_s):
    s = xm_ref.shape[2]
    d = xm_ref.shape[3]
    dh = d // ML_HEADS
    n_chunks = s // CHUNK
    nh = ML_HEADS

    ri = lax.broadcasted_iota(I32, (CHUNK, CHUNK), 0)
    ci = lax.broadcasted_iota(I32, (CHUNK, CHUNK), 1)
    tril = ri >= ci
    triu = ri <= ci
    tril_b = jnp.where(tril, 1.0, 0.0).astype(BF16)
    triu_b = jnp.where(triu, 1.0, 0.0).astype(BF16)

    pad_s[0:CONV_PAD, :] = jnp.zeros((CONV_PAD, d), F32)
    pad_s[s + CONV_PAD:s + 2 * CONV_PAD, :] = jnp.zeros((CONV_PAD, d), F32)

    def fill(c, carry):
        pad_s[pl.ds(pl.multiple_of(c * CHUNK, CHUNK) + CONV_PAD, CHUNK), :] = (
            xm_ref[0, 0, _rows(c), :].astype(F32))
        return carry
    lax.fori_loop(0, n_chunks, fill, 0)

    lane = lax.broadcasted_iota(I32, (s, CHUNK), 1)
    v_s[:, dh:dh + CHUNK] = jnp.where(lane == 0, 1.0, 0.0).astype(BF16)

    def conv_gates(c, carry):
        base = pl.multiple_of(c * CHUNK, CHUNK)
        gc = jnp.zeros((CHUNK, 4 * nh), F32) + bg_ref[...]
        gr = jnp.zeros((4 * nh, CHUNK), F32) + bgt_ref[...]
        for hh in range(nh):
            cs = slice(hh * dh, (hh + 1) * dh)
            acc = jnp.zeros((CHUNK, dh), F32) + cb_ref[:, cs]
            win = pad_s[pl.ds(base, CHUNK + 2 * CONV_PAD), cs]
            for t in range(ML_CONV):
                off = CONV_PAD - ML_CONV // 2 + t
                acc = acc + win[off:off + CHUNK, :] * cw_ref[t:t + 1, cs]
            xc = (acc * jax.nn.sigmoid(acc)).astype(BF16)
            xc_s[_rows(c), cs] = xc
            gc = gc + _dot(xc, wg_ref[cs, :])
            gr = gr + _dot_nt(wgt_ref[:, cs], xc)
        lf_c = _log_sigmoid(gc[:, nh:2 * nh])
        lb_c = _log_sigmoid(gc[:, 3 * nh:4 * nh])
        lf_r = _log_sigmoid(gr[nh:2 * nh, :])
        lb_r = _log_sigmoid(gr[3 * nh:4 * nh, :])
        bcf = sum(_dot(tril_b, p) for p in _split3(lf_c))
        bcb = sum(_dot(triu_b, p) for p in _split3(lb_c))
        brf = sum(_dot(p, triu_b) for p in _split3(lf_r))
        brb = sum(_dot(p, tril_b) for p in _split3(lb_r))
        gc_s[_rows(c), 0:nh] = gc[:, 0:nh]
        gc_s[_rows(c), nh:2 * nh] = bcf
        gc_s[_rows(c), 2 * nh:3 * nh] = gc[:, 2 * nh:3 * nh]
        gc_s[_rows(c), 3 * nh:4 * nh] = bcb
        gr_s[0:nh, _rows(c)] = gr[0:nh, :]
        gr_s[nh:2 * nh, _rows(c)] = brf
        gr_s[2 * nh:3 * nh, _rows(c)] = gr[2 * nh:3 * nh, :]
        gr_s[3 * nh:4 * nh, _rows(c)] = brb
        return carry
    lax.fori_loop(0, n_chunks, conv_gates, 0)

    for hh in range(nh):
        cs = slice(hh * dh, (hh + 1) * dh)

        def qkv(c, carry):
            xc = xc_s[_rows(c), cs]
            q_s[_rows(c), :] = _dot(xc, wq_ref[hh]).astype(BF16)
            k_s[_rows(c), :] = (_dot(xc, wk_ref[hh]) * (dh ** -0.5)).astype(BF16)
            v_s[_rows(c), 0:dh] = _dot(xm_ref[0, 0, _rows(c), cs], wv_ref[hh]).astype(BF16)
            return carry
        lax.fori_loop(0, n_chunks, qkv, 0)

        def step(c, m, i_col, b_col, mask, last_row):
            r = _rows(c)
            qc, kc, vx = q_s[r, :], k_s[r, :], v_s[r, :]
            bc = gc_s[r, b_col:b_col + 1]
            ic = gc_s[r, i_col:i_col + 1]
            br = gr_s[b_col:b_col + 1, r]
            ir = gr_s[i_col:i_col + 1, r]
            dm = jnp.where(mask, bc - br + ir, NEG)
            inter = bc + m
            m_t = jnp.maximum(inter, jnp.max(dm, axis=-1, keepdims=True))
            w_intra = jnp.exp(dm - m_t)
            w_inter = jnp.exp(inter - m_t)
            sc = (_dot_nt(qc, kc) * w_intra).astype(BF16)
            res = _dot(sc, vx) + w_inter * _dot(qc, ct_s[...].astype(BF16))
            num = res[:, 0:dh]
            den = res[:, dh:dh + 1]
            h = num / jnp.maximum(jnp.abs(den), jnp.exp(-m_t))
            b_l = bc[last_row:last_row + 1, :]
            g_col = b_l - bc + ic
            m_new = jnp.maximum(b_l + m, jnp.max(g_col, axis=0, keepdims=True))
            w_s = jnp.exp(g_col - m_new)
            w_c = jnp.exp(b_l + m - m_new)
            ct_s[...] = w_c * ct_s[...] + _dot_tn(kc, (w_s * vx.astype(F32)).astype(BF16))
            return h, m_new

        ct_s[...] = jnp.zeros(ct_s.shape, F32)

        def fwd(c, m):
            h, m_new = step(c, m, hh, nh + hh, tril, CHUNK - 1)
            hf_s[_rows(c), :] = h
            return m_new
        lax.fori_loop(0, n_chunks, fwd, jnp.zeros((1, 1), F32))

        ct_s[...] = jnp.zeros(ct_s.shape, F32)

        def bwd(i, m):
            c = n_chunks - 1 - i
            h, m_new = step(c, m, 2 * nh + hh, 3 * nh + hh, triu, 0)
            h = h + hf_s[_rows(c), :]
            y = _rms(h, hg_ref[:, cs])
            xc = xc_s[_rows(c), cs].astype(F32)
            so = so_ref[0, 0, _rows(c), cs].astype(F32)
            out_ref[0, _rows(c), cs] = (so * (y + sk_ref[:, cs] * xc)).astype(BF16)
            return m_new
        lax.fori_loop(0, n_chunks, bwd, jnp.zeros((1, 1), F32))


def _mlstm(z_all, cw, cb, wq, wk, wv, wg, wgt, bg, bgt, hg, sk):
    _, b, s, d = z_all.shape
    dh = d // ML_HEADS
    ng = 4 * ML_HEADS

    def full(a):
        return pl.BlockSpec(a.shape, lambda i, _n=a.ndim: (0,) * _n)

    return pl.pallas_call(
        _mlstm_kernel,
        grid=(b,),
        in_specs=[
            pl.BlockSpec((1, 1, s, d), lambda i: (1, i, 0, 0)),
            pl.BlockSpec((1, 1, s, d), lambda i: (2, i, 0, 0)),
            full(cw), full(cb), full(wq), full(wk), full(wv), full(wg), full(wgt),
            full(bg), full(bgt), full(hg), full(sk),
        ],
        out_specs=pl.BlockSpec((1, s, d), lambda i: (i, 0, 0)),
        out_shape=jax.ShapeDtypeStruct((b, s, d), BF16),
        scratch_shapes=[
            pltpu.VMEM((s + 2 * CONV_PAD, d), F32),
            pltpu.VMEM((s, d), BF16),
            pltpu.VMEM((s, ng), F32),
            pltpu.VMEM((ng, s), F32),
            pltpu.VMEM((s, dh), BF16),
            pltpu.VMEM((s, dh), BF16),
            pltpu.VMEM((s, dh + CHUNK), BF16),
            pltpu.VMEM((s, dh), F32),
            pltpu.VMEM((dh, dh + CHUNK), F32),
        ],
        compiler_params=_cparams("arbitrary"),
        name="mlstm",
    )(z_all, z_all, cw, cb, wq, wk, wv, wg, wgt, bg, bgt, hg, sk)


def _merge_kernel(x_ref, ya_ref, yb_ref, ga_ref, gb_ref, wa_ref, wb_ref, wo_ref, out_ref):
    merged = (ga_ref[0].astype(F32) * _dot(ya_ref[0], wa_ref[...])
              + gb_ref[0].astype(F32) * _dot(yb_ref[...], wb_ref[...]))
    out_ref[...] = x_ref[...] + _dot(merged.astype(BF16), wo_ref[...])


def _merge(x2, z_all, yb2, wa, wb, wo, tm):
    t, d = x2.shape
    z3 = z_all.reshape(z_all.shape[0], t, d)

    def slab(k):
        return pl.BlockSpec((1, tm, d), lambda i, _k=k: (_k, i, 0))

    row = pl.BlockSpec((tm, d), lambda i: (i, 0))
    wspec = pl.BlockSpec((d, d), lambda i: (0, 0))
    return pl.pallas_call(
        _merge_kernel,
        grid=(t // tm,),
        in_specs=[row, slab(0), row, slab(3), slab(4), wspec, wspec, wspec],
        out_specs=row,
        out_shape=jax.ShapeDtypeStruct((t, d), F32),
        compiler_params=_cparams("arbitrary"),
        name="merge",
    )(x2, z3, yb2, z3, z3, wa, wb, wo)


def _peer_route_kernel(x_ref, g_ref, wq_ref, keys_ref, xn_ref, idx_ref, gw_ref,
                       q_s, sv_s, si_s, cv_s, ce_s, it_s, gt_s):
    tm = x_ref.shape[0]
    k = PEER_TOPK
    big = 1e9

    xn = _rms(x_ref[...], g_ref[...])
    xn_ref[...] = xn
    q_s[...] = _dot(xn.astype(BF16), wq_ref[...]).astype(BF16)

    def sub_topk(hp, carry):
        qh = q_s[:, pl.ds(pl.multiple_of(hp * CHUNK, CHUNK), CHUNK)]
        sc = _dot_nt(keys_ref[hp], qh)
        rowf = lax.broadcasted_iota(I32, sc.shape, 0).astype(F32)
        for j in range(k):
            m = jnp.max(sc, axis=0, keepdims=True)
            i = jnp.min(jnp.where(sc == m, rowf, big), axis=0, keepdims=True)
            sv_s[hp, j:j + 1, :] = m
            si_s[hp, j:j + 1, :] = i
            sc = jnp.where(rowf == i, -jnp.inf, sc)
        return carry
    lax.fori_loop(0, 2 * PEER_HEADS, sub_topk, 0)

    def head_topk(h, carry):
        sv0, sv1 = sv_s[2 * h], sv_s[2 * h + 1]
        si0, si1 = si_s[2 * h], si_s[2 * h + 1]
        cand = jnp.concatenate([sv0[a:a + 1, :] + sv1 for a in range(k)], axis=0)
        eid = jnp.concatenate([si0[a:a + 1, :] * float(CHUNK) + si1 for a in range(k)], axis=0)
        rowf = lax.broadcasted_iota(I32, cand.shape, 0).astype(F32)
        for j in range(k):
            m = jnp.max(cand, axis=0, keepdims=True)
            i = jnp.min(jnp.where(cand == m, rowf, big), axis=0, keepdims=True)
            hit = rowf == i
            cv_s[j:j + 1, :] = m
            ce_s[j:j + 1, :] = jnp.sum(jnp.where(hit, eid, 0.0), axis=0, keepdims=True)
            cand = jnp.where(hit, -jnp.inf, cand)
        cv = cv_s[...]
        e = jnp.exp(cv - cv[0:1, :])
        rows = pl.ds(pl.multiple_of(h * k, k), k)
        gt_s[rows, :] = e / jnp.sum(e, axis=0, keepdims=True)
        it_s[rows, :] = ce_s[...]
        return carry
    lax.fori_loop(0, PEER_HEADS, head_topk, 0)

    idx_ref[...] = it_s[...].T.astype(I32)
    gw_ref[...] = gt_s[...].T


def _peer_route(x2, g, wq, keys, tm):
    t, d = x2.shape
    qd = wq.shape[1]
    nsel = PEER_HEADS * PEER_TOPK
    row = pl.BlockSpec((tm, d), lambda i: (i, 0))
    sel = pl.BlockSpec((tm, nsel), lambda i: (i, 0))
    return pl.pallas_call(
        _peer_route_kernel,
        grid=(t // tm,),
        in_specs=[
            row,
            pl.BlockSpec((1, d), lambda i: (0, 0)),
            pl.BlockSpec((d, qd), lambda i: (0, 0)),
            pl.BlockSpec(keys.shape, lambda i: (0, 0, 0)),
        ],
        out_specs=[row, sel, sel],
        out_shape=[
            jax.ShapeDtypeStruct((t, d), F32),
            jax.ShapeDtypeStruct((t, nsel), I32),
            jax.ShapeDtypeStruct((t, nsel), F32),
        ],
        scratch_shapes=[
            pltpu.VMEM((tm, qd), BF16),
            pltpu.VMEM((2 * PEER_HEADS, PEER_TOPK, tm), F32),
            pltpu.VMEM((2 * PEER_HEADS, PEER_TOPK, tm), F32),
            pltpu.VMEM((PEER_TOPK, tm), F32),
            pltpu.VMEM((PEER_TOPK, tm), F32),
            pltpu.VMEM((nsel, tm), F32),
            pltpu.VMEM((nsel, tm), F32),
        ],
        compiler_params=_cparams("arbitrary"),
        name="peer_route",
    )(x2, g, wq, keys)


SC_LANES = 16
SC_TOK_BLOCK = 8
SC_SLOTS = 4
SC_AHEAD = 3
PACK = 2 * SC_LANES


def _pack_rows(tab):
    e, d = tab.shape
    bits = lax.bitcast_convert_type(tab.astype(BF16), jnp.uint16).astype(jnp.uint32)
    bits = bits.reshape(e, d // PACK, 2, SC_LANES)
    words = bits[:, :, 0, :] | (bits[:, :, 1, :] << 16)
    return lax.bitcast_convert_type(words.reshape(e, d // 2), I32)


def _unpack_words(w):
    lo = lax.bitcast_convert_type(lax.shift_left(w, jnp.full(w.shape, 16, I32)), F32)
    hi = lax.bitcast_convert_type(w & jnp.full(w.shape, -65536, I32), F32)
    return lo, hi


def _mul_packed(a_words, b_words):
    p = plsc.bitcast(a_words, BF16) * plsc.bitcast(b_words, BF16)
    return _unpack_words(plsc.bitcast(p, I32))


def _round_to_packed_pair(x):
    bits = lax.bitcast_convert_type(x, I32)
    odd = lax.shift_right_logical(bits, jnp.full(x.shape, 16, I32)) & jnp.full(x.shape, 1, I32)
    hi = (bits + jnp.full(x.shape, 0x7FFF, I32) + odd) & jnp.full(x.shape, -65536, I32)
    return hi | lax.shift_right_logical(hi, jnp.full(x.shape, 16, I32))


def _peer_experts(u_pk, v_pk, idx, gw, xn_pk, xres):
    t, d = xres.shape
    k = PEER_TOPK
    nh = PEER_HEADS
    dw = d // 2
    assert k == SC_LANES and nh % SC_SLOTS == 0 and SC_AHEAD < SC_SLOTS and d % PACK == 0
    info = plsc.get_sparse_core_info()
    nc, ns = info.num_cores, info.num_subcores
    nw = nc * ns
    tb = SC_TOK_BLOCK
    assert t % (nw * tb) == 0
    tpw = t // nw
    n_groups = d // PACK
    mesh = plsc.VectorSubcoreMesh(core_axis_name="c", subcore_axis_name="s")
    idx2 = idx.reshape(t * nh, k)
    gw2 = gw.reshape(t * nh, k)

    @functools.partial(
        pl.kernel,
        mesh=mesh,
        out_type=jax.ShapeDtypeStruct((t, d), F32),
        scratch_types=[
            pltpu.VMEM((tb * nh, k), I32),
            pltpu.VMEM((tb * nh, k), F32),
            pltpu.VMEM((tb, dw), I32),
            pltpu.VMEM((tb, d), F32),
            pltpu.VMEM((SC_SLOTS, k, dw), I32),
            pltpu.VMEM((SC_SLOTS, k, dw), I32),
            pltpu.VMEM((k, SC_LANES), F32),
            pltpu.VMEM((k,), I32),
            pltpu.SemaphoreType.DMA((SC_SLOTS,)),
            pltpu.SemaphoreType.DMA((SC_SLOTS,)),
        ],
        compiler_params=pltpu.CompilerParams(needs_layout_passes=False),
        name="peer_experts",
    )
    def run(u_hbm, v_hbm, idx_hbm, gw_hbm, xn_hbm, xres_hbm, out_hbm,
            idx_v, gw_v, xn_v, out_v, ub, vb, red_v, w_v, usem, vsem):
        wid = lax.axis_index("s") * nc + lax.axis_index("c")
        tok0 = wid * tpw
        lane = lax.iota(I32, SC_LANES)

        def gathers(tok, h):
            row = tok * nh + h
            slot = h % SC_SLOTS
            return (pltpu.make_async_copy(u_hbm.at[idx_v.at[row]], ub.at[slot], usem.at[slot]),
                    pltpu.make_async_copy(v_hbm.at[idx_v.at[row]], vb.at[slot], vsem.at[slot]))

        def start(tok, h):
            cu, cv = gathers(tok, h)
            cu.start()
            cv.start()

        @pl.loop(0, tpw // tb)
        def _(blk):
            base = tok0 + blk * tb
            pltpu.sync_copy(idx_hbm.at[pl.ds(base * nh, tb * nh)], idx_v)
            pltpu.sync_copy(gw_hbm.at[pl.ds(base * nh, tb * nh)], gw_v)
            pltpu.sync_copy(xn_hbm.at[pl.ds(base, tb)], xn_v)
            pltpu.sync_copy(xres_hbm.at[pl.ds(base, tb)], out_v)
            for h in range(SC_AHEAD):
                start(0, h)

            @pl.loop(0, tb)
            def _(tok):
                for h in range(nh):
                    slot = h % SC_SLOTS
                    ahead = h + SC_AHEAD
                    if ahead < nh:
                        start(tok, ahead)
                    else:
                        @pl.when(tok + 1 < tb)
                        def _():
                            start(tok + 1, ahead - nh)
                    cu, cv = gathers(tok, h)
                    cu.wait()

                    def dot_rows(c, accs):
                        wo = pl.multiple_of(c * SC_LANES, SC_LANES)
                        xw = xn_v[tok, pl.ds(wo, SC_LANES)]
                        out = []
                        for r in range(k):
                            lo, hi = _mul_packed(ub[slot, r, pl.ds(wo, SC_LANES)], xw)
                            out.append(accs[r] + (lo + hi))
                        return tuple(out)
                    accs = lax.fori_loop(
                        0, n_groups, dot_rows,
                        tuple(jnp.zeros((SC_LANES,), F32) for _ in range(k)))
                    for r in range(k):
                        red_v[r, :] = accs[r]
                    cols = [plsc.load_gather(red_v, [lane, jnp.full((SC_LANES,), jj, I32)])
                            for jj in range(SC_LANES)]
                    while len(cols) > 1:
                        cols = [cols[i] + cols[i + 1] for i in range(0, len(cols), 2)]
                    act = cols[0]
                    z = GELU_C * (act + 0.044715 * act * act * act)
                    tanh_z = 1.0 - 2.0 / (jnp.exp(2.0 * z) + 1.0)
                    w_v[...] = _round_to_packed_pair(gw_v[tok * nh + h, :] * (0.5 * act * (1.0 + tanh_z)))
                    ws = [plsc.load_gather(w_v, [jnp.full((SC_LANES,), r, I32)]) for r in range(k)]

                    cv.wait()

                    @plsc.parallel_loop(0, n_groups)
                    def _(c):
                        xo = pl.multiple_of(c * PACK, PACK)
                        wo = pl.multiple_of(c * SC_LANES, SC_LANES)
                        los, his = [], []
                        for r in range(k):
                            lo, hi = _mul_packed(vb[slot, r, pl.ds(wo, SC_LANES)], ws[r])
                            los.append(lo)
                            his.append(hi)
                        while len(los) > 1:
                            los = [los[i] + los[i + 1] for i in range(0, len(los), 2)]
                            his = [his[i] + his[i + 1] for i in range(0, len(his), 2)]
                        plsc.addupdate(out_v.at[tok, pl.ds(xo, SC_LANES)], los[0])
                        plsc.addupdate(out_v.at[tok, pl.ds(xo + SC_LANES, SC_LANES)], his[0])

            pltpu.sync_copy(out_v, out_hbm.at[pl.ds(base, tb)])

    return run(u_pk, v_pk, idx2, gw2, xn_pk, xres)


def _final_norm_kernel(x_ref, g_ref, out_ref):
    out_ref[...] = _rms(x_ref[...], g_ref[...])


def _final_norm(x2, g, tm):
    t, d = x2.shape
    row = pl.BlockSpec((tm, d), lambda i: (i, 0))
    return pl.pallas_call(
        _final_norm_kernel,
        grid=(t // tm,),
        in_specs=[row, pl.BlockSpec((1, d), lambda i: (0, 0))],
        out_specs=row,
        out_shape=jax.ShapeDtypeStruct((t, d), F32),
        compiler_params=_cparams("arbitrary"),
        name="final_norm",
    )(x2, g)


def _row_tile(t, want):
    tm = min(want, t)
    assert t % tm == 0
    return tm


def kernel(x, norm_mix_g, w_in, gm_norm_g, gm_w_s, gm_b_s, ml_conv_w, ml_conv_b, ml_w_q, ml_w_k, ml_w_v, ml_w_gate, ml_b_gate, ml_head_g, ml_skip, w_branch_a, w_branch_b, w_out, norm_ffn_g, peer_w_query, peer_sub_keys, peer_u, peer_v, final_g):
    b, s, d = x.shape
    depth = w_in.shape[0]
    assert s % CHUNK == 0 and d % (ML_HEADS * CHUNK) == 0 and w_in.shape[2] == N_IN_SEG * d
    n_groups = BATCH_GROUPS if b % BATCH_GROUPS == 0 else 1
    bg = b // n_groups
    t = bg * s
    tm_merge = _row_tile(t, 512)
    tm_route = _row_tile(t, 256)
    parts = [x[i * bg:(i + 1) * bg].reshape(t, d) for i in range(n_groups)]
    for l in range(depth):
        w_in_b = w_in[l].astype(BF16)
        w_s_b = gm_w_s[l].astype(BF16)
        wq_b, wk_b, wv_b = (ml_w_q[l].astype(BF16), ml_w_k[l].astype(BF16), ml_w_v[l].astype(BF16))
        wg = ml_w_gate[l].astype(BF16)
        wa_b, wb_b, wo_b = (w_branch_a[l].astype(BF16), w_branch_b[l].astype(BF16),
                            w_out[l].astype(BF16))
        wquery_b = peer_w_query[l].astype(BF16)
        keys = peer_sub_keys[l].reshape(2 * PEER_HEADS, CHUNK, -1).astype(BF16)
        u_pk, v_pk = _pack_rows(peer_u[l]), _pack_rows(peer_v[l])
        for p in range(n_groups):
            x2 = parts[p]
            z_all = _in_proj(
                x2.reshape(bg, s, d), norm_mix_g[l].reshape(1, d), w_in_b,
                gm_norm_g[l].reshape(1, d), w_s_b, gm_b_s[l].T)
            yb = _mlstm(
                z_all, ml_conv_w[l], ml_conv_b[l].reshape(1, d), wq_b, wk_b, wv_b,
                wg, wg.T, ml_b_gate[l].reshape(1, -1), ml_b_gate[l].reshape(-1, 1),
                ml_head_g[l].reshape(1, d), ml_skip[l].reshape(1, d))
            x2 = _merge(x2, z_all, yb.reshape(t, d), wa_b, wb_b, wo_b, tm_merge)
            xn, idx, gw = _peer_route(x2, norm_ffn_g[l].reshape(1, d), wquery_b, keys, tm_route)
            parts[p] = _peer_experts(u_pk, v_pk, idx, gw, _pack_rows(xn), x2)
    outs = [_final_norm(p, final_g.reshape(1, d), tm_merge) for p in parts]
    return jnp.concatenate(outs, axis=0).reshape(b, s, d)
```

```python
import functools

import jax
import jax.numpy as jnp
from jax import lax
from jax.experimental import pallas as pl
from jax.experimental.pallas import tpu as pltpu
from jax.experimental.pallas import tpu_sc as plsc

F32 = jnp.float32
BF16 = jnp.bfloat16
I32 = jnp.int32

EPS = 1e-6
NEG = -1e30
CHUNK = 128
GM_GROUPS = 8
ML_HEADS = 4
ML_CONV = 5
CONV_PAD = 8
PEER_HEADS = 8
PEER_TOPK = 16
N_IN_SEG = 6
BATCH_GROUPS = 2
GELU_C = 0.7978845608028654

VMEM_LIMIT = 56 * 1024 * 1024


def _cparams(*sem):
    return pltpu.CompilerParams(dimension_semantics=sem, vmem_limit_bytes=VMEM_LIMIT)


def _rows(c, n=CHUNK):
    return pl.ds(pl.multiple_of(c * n, n), n)


def _rms(x, g):
    return x * lax.rsqrt(jnp.mean(x * x, axis=-1, keepdims=True) + EPS) * g


def _dot(a, b):
    return jnp.dot(a, b, preferred_element_type=F32)


def _dot_nt(a, b):
    return lax.dot_general(a, b, (((1,), (1,)), ((), ())), preferred_element_type=F32)


def _dot_tn(a, b):
    return lax.dot_general(a, b, (((0,), (0,)), ((), ())), preferred_element_type=F32)


def _in_proj_kernel(x_ref, g_ref, w_ref, gmg_ref, ws_ref, bst_ref, out_ref, hn_s, gu_s):
    j = pl.program_id(1)
    n_chunks = x_ref.shape[1] // CHUNK
    d = x_ref.shape[2]

    def z_of(c):
        return _dot(hn_s[_rows(c), :], w_ref[...])

    def for_chunks(fn):
        def body(c, carry):
            fn(c)
            return carry
        lax.fori_loop(0, n_chunks, body, 0)

    @pl.when(j == 0)
    def _():
        def norm(c):
            hn_s[_rows(c), :] = _rms(x_ref[0, _rows(c), :], g_ref[...]).astype(BF16)
        for_chunks(norm)

        def seg_u(c):
            gu_s[_rows(c), :] = jax.nn.gelu(z_of(c)).astype(BF16)
        for_chunks(seg_u)

    @pl.when(j == 1)
    def _():
        def seg_v(c):
            vn = _rms(jax.nn.gelu(z_of(c)), gmg_ref[...]).astype(BF16)
            gu = gu_s[_rows(c), :]
            for g in range(GM_GROUPS):
                cs = slice(g * (d // GM_GROUPS), (g + 1) * (d // GM_GROUPS))
                mixed = _dot(ws_ref[g], vn[:, cs]) + bst_ref[:, g:g + 1]
                out_ref[0, 0, _rows(c), cs] = (gu[:, cs].astype(F32) * mixed).astype(BF16)
        for_chunks(seg_v)

    @pl.when(j == 2)
    def _():
        def seg_xm(c):
            out_ref[0, 0, _rows(c), :] = z_of(c).astype(BF16)
        for_chunks(seg_xm)

    @pl.when(j >= 3)
    def _():
        def seg_gate(c):
            out_ref[0, 0, _rows(c), :] = jax.nn.sigmoid(z_of(c)).astype(BF16)
        for_chunks(seg_gate)


def _in_proj(x3, g, w_in, gm_g, w_s, b_st):
    b, s, d = x3.shape
    return pl.pallas_call(
        _in_proj_kernel,
        grid=(b, N_IN_SEG),
        in_specs=[
            pl.BlockSpec((1, s, d), lambda i, j: (i, 0, 0)),
            pl.BlockSpec((1, d), lambda i, j: (0, 0)),
            pl.BlockSpec((d, d), lambda i, j: (0, j)),
            pl.BlockSpec((1, d), lambda i, j: (0, 0)),
            pl.BlockSpec((GM_GROUPS, CHUNK, CHUNK), lambda i, j: (0, 0, 0)),
            pl.BlockSpec((CHUNK, GM_GROUPS), lambda i, j: (0, 0)),
        ],
        out_specs=pl.BlockSpec((1, 1, s, d), lambda i, j: (jnp.maximum(j - 1, 0), i, 0, 0)),
        out_shape=jax.ShapeDtypeStruct((N_IN_SEG - 1, b, s, d), BF16),
        scratch_shapes=[pltpu.VMEM((s, d), BF16), pltpu.VMEM((s, d), BF16)],
        compiler_params=_cparams("arbitrary", "arbitrary"),
        name="in_proj",
    )(x3, g, w_in, gm_g, w_s, b_st)


def _split3(a):
    hi = a.astype(BF16)
    r1 = a - hi.astype(F32)
    mid = r1.astype(BF16)
    lo = (r1 - mid.astype(F32)).astype(BF16)
    return hi, mid, lo


def _log_sigmoid(x):
    return jnp.minimum(x, 0.0) - jnp.log(1.0 + jnp.exp(-jnp.abs(x)))


def _mlstm_kernel(xm_ref, so_ref, cw_ref, cb_ref, wq_ref, wk_ref, wv_ref, wg_ref, wgt_ref,
                  bg_ref, bgt_ref, hg_ref, sk_ref, out_ref,
                  pad_s, xc_s, gc_s, gr_s, q_s, k_s, v_s, hf_s, ct_s):
    s = xm_ref.shape[2]
    d = xm_ref.shape[3]
    dh = d // ML_HEADS
    n_chunks = s // CHUNK
    nh = ML_HEADS

    ri = lax.broadcasted_iota(I32, (CHUNK, CHUNK), 0)
    ci = lax.broadcasted_iota(I32, (CHUNK, CHUNK), 1)
    tril = ri >= ci
    triu = ri <= ci
    tril_b = jnp.where(tril, 1.0, 0.0).astype(BF16)
    triu_b = jnp.where(triu, 1.0, 0.0).astype(BF16)

    pad_s[0:CONV_PAD, :] = jnp.zeros((CONV_PAD, d), F32)
    pad_s[s + CONV_PAD:s + 2 * CONV_PAD, :] = jnp.zeros((CONV_PAD, d), F32)

    def fill(c, carry):
        pad_s[pl.ds(pl.multiple_of(c * CHUNK, CHUNK) + CONV_PAD, CHUNK), :] = (
            xm_ref[0, 0, _rows(c), :].astype(F32))
        return carry
    lax.fori_loop(0, n_chunks, fill, 0)

    lane = lax.broadcasted_iota(I32, (s, CHUNK), 1)
    v_s[:, dh:dh + CHUNK] = jnp.where(lane == 0, 1.0, 0.0).astype(BF16)

    def conv_gates(c, carry):
        base = pl.multiple_of(c * CHUNK, CHUNK)
        gc = jnp.zeros((CHUNK, 4 * nh), F32) + bg_ref[...]
        gr = jnp.zeros((4 * nh, CHUNK), F32) + bgt_ref[...]
        for hh in range(nh):
            cs = slice(hh * dh, (hh + 1) * dh)
            acc = jnp.zeros((CHUNK, dh), F32) + cb_ref[:, cs]
            win = pad_s[pl.ds(base, CHUNK + 2 * CONV_PAD), cs]
            for t in range(ML_CONV):
                off = CONV_PAD - ML_CONV // 2 + t
                acc = acc + win[off:off + CHUNK, :] * cw_ref[t:t + 1, cs]
            xc = (acc * jax.nn.sigmoid(acc)).astype(BF16)
            xc_s[_rows(c), cs] = xc
            gc = gc + _dot(xc, wg_ref[cs, :])
            gr = gr + _dot_nt(wgt_ref[:, cs], xc)
        lf_c = _log_sigmoid(gc[:, nh:2 * nh])
        lb_c = _log_sigmoid(gc[:, 3 * nh:4 * nh])
        lf_r = _log_sigmoid(gr[nh:2 * nh, :])
        lb_r = _log_sigmoid(gr[3 * nh:4 * nh, :])
        bcf = sum(_dot(tril_b, p) for p in _split3(lf_c))
        bcb = sum(_dot(triu_b, p) for p in _split3(lb_c))
        brf = sum(_dot(p, triu_b) for p in _split3(lf_r))
        brb = sum(_dot(p, tril_b) for p in _split3(lb_r))
        gc_s[_rows(c), 0:nh] = gc[:, 0:nh]
        gc_s[_rows(c), nh:2 * nh] = bcf
        gc_s[_rows(c), 2 * nh:3 * nh] = gc[:, 2 * nh:3 * nh]
        gc_s[_rows(c), 3 * nh:4 * nh] = bcb
        gr_s[0:nh, _rows(c)] = gr[0:nh, :]
        gr_s[nh:2 * nh, _rows(c)] = brf
        gr_s[2 * nh:3 * nh, _rows(c)] = gr[2 * nh:3 * nh, :]
        gr_s[3 * nh:4 * nh, _rows(c)] = brb
        return carry
    lax.fori_loop(0, n_chunks, conv_gates, 0)

    for hh in range(nh):
        cs = slice(hh * dh, (hh + 1) * dh)

        def qkv(c, carry):
            xc = xc_s[_rows(c), cs]
            q_s[_rows(c), :] = _dot(xc, wq_ref[hh]).astype(BF16)
            k_s[_rows(c), :] = (_dot(xc, wk_ref[hh]) * (dh ** -0.5)).astype(BF16)
            v_s[_rows(c), 0:dh] = _dot(xm_ref[0, 0, _rows(c), cs], wv_ref[hh]).astype(BF16)
            return carry
        lax.fori_loop(0, n_chunks, qkv, 0)

        def step(c, m, i_col, b_col, mask, last_row):
            r = _rows(c)
            qc, kc, vx = q_s[r, :], k_s[r, :], v_s[r, :]
            bc = gc_s[r, b_col:b_col + 1]
            ic = gc_s[r, i_col:i_col + 1]
            br = gr_s[b_col:b_col + 1, r]
            ir = gr_s[i_col:i_col + 1, r]
            dm = jnp.where(mask, bc - br + ir, NEG)
            inter = bc + m
            m_t = jnp.maximum(inter, jnp.max(dm, axis=-1, keepdims=True))
            w_intra = jnp.exp(dm - m_t)
            w_inter = jnp.exp(inter - m_t)
            sc = (_dot_nt(qc, kc) * w_intra).astype(BF16)
            res = _dot(sc, vx) + w_inter * _dot(qc, ct_s[...].astype(BF16))
            num = res[:, 0:dh]
            den = res[:, dh:dh + 1]
            h = num / jnp.maximum(jnp.abs(den), jnp.exp(-m_t))
            b_l = bc[last_row:last_row + 1, :]
            g_col = b_l - bc + ic
            m_new = jnp.maximum(b_l + m, jnp.max(g_col, axis=0, keepdims=True))
            w_s = jnp.exp(g_col - m_new)
            w_c = jnp.exp(b_l + m - m_new)
            ct_s[...] = w_c * ct_s[...] + _dot_tn(kc, (w_s * vx.astype(F32)).astype(BF16))
            return h, m_new

        ct_s[...] = jnp.zeros(ct_s.shape, F32)

        def fwd(c, m):
            h, m_new = step(c, m, hh, nh + hh, tril, CHUNK - 1)
            hf_s[_rows(c), :] = h
            return m_new
        lax.fori_loop(0, n_chunks, fwd, jnp.zeros((1, 1), F32))

        ct_s[...] = jnp.zeros(ct_s.shape, F32)

        def bwd(i, m):
            c = n_chunks - 1 - i
            h, m_new = step(c, m, 2 * nh + hh, 3 * nh + hh, triu, 0)
            h = h + hf_s[_rows(c), :]
            y = _rms(h, hg_ref[:, cs])
            xc = xc_s[_rows(c), cs].astype(F32)
            so = so_ref[0, 0, _rows(c), cs].astype(F32)
            out_ref[0, _rows(c), cs] = (so * (y + sk_ref[:, cs] * xc)).astype(BF16)
            return m_new
        lax.fori_loop(0, n_chunks, bwd, jnp.zeros((1, 1), F32))


def _mlstm(z_all, cw, cb, wq, wk, wv, wg, wgt, bg, bgt, hg, sk):
    _, b, s, d = z_all.shape
    dh = d // ML_HEADS
    ng = 4 * ML_HEADS

    def full(a):
        return pl.BlockSpec(a.shape, lambda i, _n=a.ndim: (0,) * _n)

    return pl.pallas_call(
        _mlstm_kernel,
        grid=(b,),
        in_specs=[
            pl.BlockSpec((1, 1, s, d), lambda i: (1, i, 0, 0)),
            pl.BlockSpec((1, 1, s, d), lambda i: (2, i, 0, 0)),
            full(cw), full(cb), full(wq), full(wk), full(wv), full(wg), full(wgt),
            full(bg), full(bgt), full(hg), full(sk),
        ],
        out_specs=pl.BlockSpec((1, s, d), lambda i: (i, 0, 0)),
        out_shape=jax.ShapeDtypeStruct((b, s, d), BF16),
        scratch_shapes=[
            pltpu.VMEM((s + 2 * CONV_PAD, d), F32),
            pltpu.VMEM((s, d), BF16),
            pltpu.VMEM((s, ng), F32),
            pltpu.VMEM((ng, s), F32),
            pltpu.VMEM((s, dh), BF16),
            pltpu.VMEM((s, dh), BF16),
            pltpu.VMEM((s, dh + CHUNK), BF16),
            pltpu.VMEM((s, dh), F32),
            pltpu.VMEM((dh, dh + CHUNK), F32),
        ],
        compiler_params=_cparams("arbitrary"),
        name="mlstm",
    )(z_all, z_all, cw, cb, wq, wk, wv, wg, wgt, bg, bgt, hg, sk)


def _merge_kernel(x_ref, ya_ref, yb_ref, ga_ref, gb_ref, wa_ref, wb_ref, wo_ref, out_ref):
    merged = (ga_ref[0].astype(F32) * _dot(ya_ref[0], wa_ref[...])
              + gb_ref[0].astype(F32) * _dot(yb_ref[...], wb_ref[...]))
    out_ref[...] = x_ref[...] + _dot(merged.astype(BF16), wo_ref[...])


def _merge(x2, z_all, yb2, wa, wb, wo, tm):
    t, d = x2.shape
    z3 = z_all.reshape(z_all.shape[0], t, d)

    def slab(k):
        return pl.BlockSpec((1, tm, d), lambda i, _k=k: (_k, i, 0))

    row = pl.BlockSpec((tm, d), lambda i: (i, 0))
    wspec = pl.BlockSpec((d, d), lambda i: (0, 0))
    return pl.pallas_call(
        _merge_kernel,
        grid=(t // tm,),
        in_specs=[row, slab(0), row, slab(3), slab(4), wspec, wspec, wspec],
        out_specs=row,
        out_shape=jax.ShapeDtypeStruct((t, d), F32),
        compiler_params=_cparams("arbitrary"),
        name="merge",
    )(x2, z3, yb2, z3, z3, wa, wb, wo)


def _peer_route_kernel(x_ref, g_ref, wq_ref, keys_ref, xn_ref, idx_ref, gw_ref,
                       q_s, sv_s, si_s, cv_s, ce_s, it_s, gt_s):
    tm = x_ref.shape[0]
    k = PEER_TOPK
    big = 1e9

    xn = _rms(x_ref[...], g_ref[...])
    xn_ref[...] = xn
    q_s[...] = _dot(xn.astype(BF16), wq_ref[...]).astype(BF16)

    def sub_topk(hp, carry):
        qh = q_s[:, pl.ds(pl.multiple_of(hp * CHUNK, CHUNK), CHUNK)]
        sc = _dot_nt(keys_ref[hp], qh)
        rowf = lax.broadcasted_iota(I32, sc.shape, 0).astype(F32)
        for j in range(k):
            m = jnp.max(sc, axis=0, keepdims=True)
            i = jnp.min(jnp.where(sc == m, rowf, big), axis=0, keepdims=True)
            sv_s[hp, j:j + 1, :] = m
            si_s[hp, j:j + 1, :] = i
            sc = jnp.where(rowf == i, -jnp.inf, sc)
        return carry
    lax.fori_loop(0, 2 * PEER_HEADS, sub_topk, 0)

    def head_topk(h, carry):
        sv0, sv1 = sv_s[2 * h], sv_s[2 * h + 1]
        si0, si1 = si_s[2 * h], si_s[2 * h + 1]
        cand = jnp.concatenate([sv0[a:a + 1, :] + sv1 for a in range(k)], axis=0)
        eid = jnp.concatenate([si0[a:a + 1, :] * float(CHUNK) + si1 for a in range(k)], axis=0)
        rowf = lax.broadcasted_iota(I32, cand.shape, 0).astype(F32)
        for j in range(k):
            m = jnp.max(cand, axis=0, keepdims=True)
            i = jnp.min(jnp.where(cand == m, rowf, big), axis=0, keepdims=True)
            hit = rowf == i
            cv_s[j:j + 1, :] = m
            ce_s[j:j + 1, :] = jnp.sum(jnp.where(hit, eid, 0.0), axis=0, keepdims=True)
            cand = jnp.where(hit, -jnp.inf, cand)
        cv = cv_s[...]
        e = jnp.exp(cv - cv[0:1, :])
        rows = pl.ds(pl.multiple_of(h * k, k), k)
        gt_s[rows, :] = e / jnp.sum(e, axis=0, keepdims=True)
        it_s[rows, :] = ce_s[...]
        return carry
    lax.fori_loop(0, PEER_HEADS, head_topk, 0)

    idx_ref[...] = it_s[...].T.astype(I32)
    gw_ref[...] = gt_s[...].T


def _peer_route(x2, g, wq, keys, tm):
    t, d = x2.shape
    qd = wq.shape[1]
    nsel = PEER_HEADS * PEER_TOPK
    row = pl.BlockSpec((tm, d), lambda i: (i, 0))
    sel = pl.BlockSpec((tm, nsel), lambda i: (i, 0))
    return pl.pallas_call(
        _peer_route_kernel,
        grid=(t // tm,),
        in_specs=[
            row,
            pl.BlockSpec((1, d), lambda i: (0, 0)),
            pl.BlockSpec((d, qd), lambda i: (0, 0)),
            pl.BlockSpec(keys.shape, lambda i: (0, 0, 0)),
        ],
        out_specs=[row, sel, sel],
        out_shape=[
            jax.ShapeDtypeStruct((t, d), F32),
            jax.ShapeDtypeStruct((t, nsel), I32),
            jax.ShapeDtypeStruct((t, nsel), F32),
        ],
        scratch_shapes=[
            pltpu.VMEM((tm, qd), BF16),
            pltpu.VMEM((2 * PEER_HEADS, PEER_TOPK, tm), F32),
            pltpu.VMEM((2 * PEER_HEADS, PEER_TOPK, tm), F32),
            pltpu.VMEM((PEER_TOPK, tm), F32),
            pltpu.VMEM((PEER_TOPK, tm), F32),
            pltpu.VMEM((nsel, tm), F32),
            pltpu.VMEM((nsel, tm), F32),
        ],
        compiler_params=_cparams("arbitrary"),
        name="peer_route",
    )(x2, g, wq, keys)


SC_LANES = 16
SC_TOK_BLOCK = 8
SC_RING = 8
SC_AHEAD = 6
PACK = 2 * SC_LANES


def _pack_rows(tab):
    e, d = tab.shape
    bits = lax.bitcast_convert_type(tab.astype(BF16), jnp.uint16).astype(jnp.uint32)
    bits = bits.reshape(e, d // PACK, 2, SC_LANES)
    words = bits[:, :, 0, :] | (bits[:, :, 1, :] << 16)
    return lax.bitcast_convert_type(words.reshape(e, d // 2), I32)


def _unpack_words(w):
    lo = lax.bitcast_convert_type(lax.shift_left(w, jnp.full(w.shape, 16, I32)), F32)
    hi = lax.bitcast_convert_type(w & jnp.full(w.shape, -65536, I32), F32)
    return lo, hi


def _mul2_packed(a0, b0, a1, b1):
    p = (plsc.bitcast(a0, BF16) * plsc.bitcast(b0, BF16)
         + plsc.bitcast(a1, BF16) * plsc.bitcast(b1, BF16))
    return _unpack_words(plsc.bitcast(p, I32))


def _round_to_packed_pair(x):
    bits = lax.bitcast_convert_type(x, I32)
    odd = lax.shift_right_logical(bits, jnp.full(x.shape, 16, I32)) & jnp.full(x.shape, 1, I32)
    hi = (bits + jnp.full(x.shape, 0x7FFF, I32) + odd) & jnp.full(x.shape, -65536, I32)
    return hi | lax.shift_right_logical(hi, jnp.full(x.shape, 16, I32))


def _peer_experts(u_pk, v_pk, idx, gw, xn_pk, xres):
    t, d = xres.shape
    k = PEER_TOPK
    nh = PEER_HEADS
    dw = d // 2
    n_steps = 2 * nh
    assert k == SC_LANES and n_steps % SC_RING == 0 and SC_AHEAD < SC_RING and d % (2 * PACK) == 0
    info = plsc.get_sparse_core_info()
    nc, ns = info.num_cores, info.num_subcores
    nw = nc * ns
    tb = SC_TOK_BLOCK
    assert t % (nw * tb) == 0
    tpw = t // nw
    n_groups = d // PACK
    mesh = plsc.VectorSubcoreMesh(core_axis_name="c", subcore_axis_name="s")
    idx2 = idx.reshape(t * nh, k)
    gw2 = gw.reshape(t * nh, k)

    @functools.partial(
        pl.kernel,
        mesh=mesh,
        out_type=jax.ShapeDtypeStruct((t, d), F32),
        scratch_types=[
            pltpu.VMEM((tb * nh, k), I32),
            pltpu.VMEM((tb * nh, k), F32),
            pltpu.VMEM((tb, dw), I32),
            pltpu.VMEM((tb, d), F32),
            pltpu.VMEM((SC_RING, k, dw), I32),
            pltpu.VMEM((k, SC_LANES), F32),
            pltpu.VMEM((nh, k), F32),
            pltpu.VMEM((nh, k), I32),
            pltpu.SemaphoreType.DMA((SC_RING,)),
        ],
        compiler_params=pltpu.CompilerParams(needs_layout_passes=False),
        name="peer_experts",
    )
    def run(u_hbm, v_hbm, idx_hbm, gw_hbm, xn_hbm, xres_hbm, out_hbm,
            idx_v, gw_v, xn_v, out_v, ring, red_v, act_v, w_v, sems):
        wid = lax.axis_index("s") * nc + lax.axis_index("c")
        tok0 = wid * tpw
        lane = lax.iota(I32, SC_LANES)

        def gather(tok, step):
            tab = u_hbm if step < nh else v_hbm
            slot = step % SC_RING
            return pltpu.make_async_copy(
                tab.at[idx_v.at[tok * nh + step % nh]], ring.at[slot], sems.at[slot])

        @pl.loop(0, tpw // tb)
        def _(blk):
            base = tok0 + blk * tb
            pltpu.sync_copy(idx_hbm.at[pl.ds(base * nh, tb * nh)], idx_v)
            pltpu.sync_copy(gw_hbm.at[pl.ds(base * nh, tb * nh)], gw_v)
            pltpu.sync_copy(xn_hbm.at[pl.ds(base, tb)], xn_v)
            pltpu.sync_copy(xres_hbm.at[pl.ds(base, tb)], out_v)
            for step in range(SC_AHEAD):
                gather(0, step).start()

            @pl.loop(0, tb)
            def _(tok):
                for step in range(n_steps):
                    h = step % nh
                    slot = step % SC_RING
                    ahead = step + SC_AHEAD
                    if ahead < n_steps:
                        gather(tok, ahead).start()
                    else:
                        @pl.when(tok + 1 < tb)
                        def _():
                            gather(tok + 1, ahead - n_steps).start()

                    if step == nh:
                        for hh in range(nh):
                            act = act_v[hh, :]
                            z = GELU_C * (act + 0.044715 * act * act * act)
                            tanh_z = 1.0 - 2.0 / (jnp.exp(2.0 * z) + 1.0)
                            w_v[hh, :] = _round_to_packed_pair(
                                gw_v[tok * nh + hh, :] * (0.5 * act * (1.0 + tanh_z)))

                    gather(tok, step).wait()

                    if step < nh:
                        def dot_rows(c, accs):
                            w0 = pl.multiple_of(c * PACK, PACK)
                            w1 = w0 + SC_LANES
                            x0 = xn_v[tok, pl.ds(w0, SC_LANES)]
                            x1 = xn_v[tok, pl.ds(w1, SC_LANES)]
                            out = []
                            for r in range(k):
                                lo, hi = _mul2_packed(ring[slot, r, pl.ds(w0, SC_LANES)], x0,
                                                      ring[slot, r, pl.ds(w1, SC_LANES)], x1)
                                out.append(accs[r] + (lo + hi))
                            return tuple(out)
                        accs = lax.fori_loop(
                            0, n_groups // 2, dot_rows,
                            tuple(jnp.zeros((SC_LANES,), F32) for _ in range(k)))
                        for r in range(k):
                            red_v[r, :] = accs[r]
                        cols = [plsc.load_gather(red_v, [lane, jnp.full((SC_LANES,), jj, I32)])
                                for jj in range(SC_LANES)]
                        while len(cols) > 1:
                            cols = [cols[i] + cols[i + 1] for i in range(0, len(cols), 2)]
                        act_v[h, :] = cols[0]
                    else:
                        wv = w_v[h, :]
                        ws = [wv.at[jnp.full((SC_LANES,), r, I32)].get(mode="promise_in_bounds")
                              for r in range(k)]

                        @plsc.parallel_loop(0, n_groups)
                        def _(c):
                            xo = pl.multiple_of(c * PACK, PACK)
                            wo = pl.multiple_of(c * SC_LANES, SC_LANES)
                            los, his = [], []
                            for r in range(0, k, 2):
                                lo, hi = _mul2_packed(ring[slot, r, pl.ds(wo, SC_LANES)], ws[r],
                                                      ring[slot, r + 1, pl.ds(wo, SC_LANES)], ws[r + 1])
                                los.append(lo)
                                his.append(hi)
                            while len(los) > 1:
                                los = [los[i] + los[i + 1] for i in range(0, len(los), 2)]
                                his = [his[i] + his[i + 1] for i in range(0, len(his), 2)]
                            plsc.addupdate(out_v.at[tok, pl.ds(xo, SC_LANES)], los[0])
                            plsc.addupdate(out_v.at[tok, pl.ds(xo + SC_LANES, SC_LANES)], his[0])

            pltpu.sync_copy(out_v, out_hbm.at[pl.ds(base, tb)])

    return run(u_pk, v_pk, idx2, gw2, xn_pk, xres)


def _final_norm_kernel(x_ref, g_ref, out_ref):
    out_ref[...] = _rms(x_ref[...], g_ref[...])


def _final_norm(x2, g, tm):
    t, d = x2.shape
    row = pl.BlockSpec((tm, d), lambda i: (i, 0))
    return pl.pallas_call(
        _final_norm_kernel,
        grid=(t // tm,),
        in_specs=[row, pl.BlockSpec((1, d), lambda i: (0, 0))],
        out_specs=row,
        out_shape=jax.ShapeDtypeStruct((t, d), F32),
        compiler_params=_cparams("arbitrary"),
        name="final_norm",
    )(x2, g)


def _row_tile(t, want):
    tm = min(want, t)
    assert t % tm == 0
    return tm


def kernel(x, norm_mix_g, w_in, gm_norm_g, gm_w_s, gm_b_s, ml_conv_w, ml_conv_b, ml_w_q, ml_w_k, ml_w_v, ml_w_gate, ml_b_gate, ml_head_g, ml_skip, w_branch_a, w_branch_b, w_out, norm_ffn_g, peer_w_query, peer_sub_keys, peer_u, peer_v, final_g):
    b, s, d = x.shape
    depth = w_in.shape[0]
    assert s % CHUNK == 0 and d % (ML_HEADS * CHUNK) == 0 and w_in.shape[2] == N_IN_SEG * d
    n_groups = BATCH_GROUPS if b % BATCH_GROUPS == 0 else 1
    bg = b // n_groups
    t = bg * s
    tm_merge = _row_tile(t, 512)
    tm_route = _row_tile(t, 256)
    parts = [x[i * bg:(i + 1) * bg].reshape(t, d) for i in range(n_groups)]
    for l in range(depth):
        w_in_b = w_in[l].astype(BF16)
        w_s_b = gm_w_s[l].astype(BF16)
        wq_b, wk_b, wv_b = (ml_w_q[l].astype(BF16), ml_w_k[l].astype(BF16), ml_w_v[l].astype(BF16))
        wg = ml_w_gate[l].astype(BF16)
        wa_b, wb_b, wo_b = (w_branch_a[l].astype(BF16), w_branch_b[l].astype(BF16),
                            w_out[l].astype(BF16))
        wquery_b = peer_w_query[l].astype(BF16)
        keys = peer_sub_keys[l].reshape(2 * PEER_HEADS, CHUNK, -1).astype(BF16)
        u_pk, v_pk = _pack_rows(peer_u[l]), _pack_rows(peer_v[l])
        for p in range(n_groups):
            x2 = parts[p]
            z_all = _in_proj(
                x2.reshape(bg, s, d), norm_mix_g[l].reshape(1, d), w_in_b,
                gm_norm_g[l].reshape(1, d), w_s_b, gm_b_s[l].T)
            yb = _mlstm(
                z_all, ml_conv_w[l], ml_conv_b[l].reshape(1, d), wq_b, wk_b, wv_b,
                wg, wg.T, ml_b_gate[l].reshape(1, -1), ml_b_gate[l].reshape(-1, 1),
                ml_head_g[l].reshape(1, d), ml_skip[l].reshape(1, d))
            x2 = _merge(x2, z_all, yb.reshape(t, d), wa_b, wb_b, wo_b, tm_merge)
            xn, idx, gw = _peer_route(x2, norm_ffn_g[l].reshape(1, d), wquery_b, keys, tm_route)
            parts[p] = _peer_experts(u_pk, v_pk, idx, gw, _pack_rows(xn), x2)
    outs = [_final_norm(p, final_g.reshape(1, d), tm_merge) for p in parts]
    return jnp.concatenate(outs, axis=0).reshape(b, s, d)
```

```python
import functools

import jax
import jax.numpy as jnp
from jax import lax
from jax.experimental import pallas as pl
from jax.experimental.pallas import tpu as pltpu
from jax.experimental.pallas import tpu_sc as plsc

F32 = jnp.float32
BF16 = jnp.bfloat16
I32 = jnp.int32

EPS = 1e-6
NEG = -1e30
CHUNK = 128
GM_GROUPS = 8
ML_HEADS = 4
ML_CONV = 5
CONV_PAD = 8
PEER_HEADS = 8
PEER_TOPK = 16
N_IN_SEG = 6
BATCH_GROUPS = 2
GELU_C = 0.7978845608028654

VMEM_LIMIT = 56 * 1024 * 1024


def _cparams(*sem):
    return pltpu.CompilerParams(dimension_semantics=sem, vmem_limit_bytes=VMEM_LIMIT)


def _rows(c, n=CHUNK):
    return pl.ds(pl.multiple_of(c * n, n), n)


def _rms(x, g):
    return x * lax.rsqrt(jnp.mean(x * x, axis=-1, keepdims=True) + EPS) * g


def _dot(a, b):
    return jnp.dot(a, b, preferred_element_type=F32)


def _dot_nt(a, b):
    return lax.dot_general(a, b, (((1,), (1,)), ((), ())), preferred_element_type=F32)


def _dot_tn(a, b):
    return lax.dot_general(a, b, (((0,), (0,)), ((), ())), preferred_element_type=F32)


def _in_proj_kernel(x_ref, g_ref, w_ref, gmg_ref, ws_ref, bst_ref, out_ref, hn_s, gu_s):
    j = pl.program_id(1)
    n_chunks = x_ref.shape[1] // CHUNK
    d = x_ref.shape[2]

    def z_of(c):
        return _dot(hn_s[_rows(c), :], w_ref[...])

    def for_chunks(fn):
        def body(c, carry):
            fn(c)
            return carry
        lax.fori_loop(0, n_chunks, body, 0)

    @pl.when(j == 0)
    def _():
        def norm(c):
            hn_s[_rows(c), :] = _rms(x_ref[0, _rows(c), :], g_ref[...]).astype(BF16)
        for_chunks(norm)

        def seg_u(c):
            gu_s[_rows(c), :] = jax.nn.gelu(z_of(c)).astype(BF16)
        for_chunks(seg_u)

    @pl.when(j == 1)
    def _():
        def seg_v(c):
            vn = _rms(jax.nn.gelu(z_of(c)), gmg_ref[...]).astype(BF16)
            gu = gu_s[_rows(c), :]
            for g in range(GM_GROUPS):
                cs = slice(g * (d // GM_GROUPS), (g + 1) * (d // GM_GROUPS))
                mixed = _dot(ws_ref[g], vn[:, cs]) + bst_ref[:, g:g + 1]
                out_ref[0, 0, _rows(c), cs] = (gu[:, cs].astype(F32) * mixed).astype(BF16)
        for_chunks(seg_v)

    @pl.when(j == 2)
    def _():
        def seg_xm(c):
            out_ref[0, 0, _rows(c), :] = z_of(c).astype(BF16)
        for_chunks(seg_xm)

    @pl.when(j >= 3)
    def _():
        def seg_gate(c):
            out_ref[0, 0, _rows(c), :] = jax.nn.sigmoid(z_of(c)).astype(BF16)
        for_chunks(seg_gate)


def _in_proj(x3, g, w_in, gm_g, w_s, b_st):
    b, s, d = x3.shape
    return pl.pallas_call(
        _in_proj_kernel,
        grid=(b, N_IN_SEG),
        in_specs=[
            pl.BlockSpec((1, s, d), lambda i, j: (i, 0, 0)),
            pl.BlockSpec((1, d), lambda i, j: (0, 0)),
            pl.BlockSpec((d, d), lambda i, j: (0, j)),
            pl.BlockSpec((1, d), lambda i, j: (0, 0)),
            pl.BlockSpec((GM_GROUPS, CHUNK, CHUNK), lambda i, j: (0, 0, 0)),
            pl.BlockSpec((CHUNK, GM_GROUPS), lambda i, j: (0, 0)),
        ],
        out_specs=pl.BlockSpec((1, 1, s, d), lambda i, j: (jnp.maximum(j - 1, 0), i, 0, 0)),
        out_shape=jax.ShapeDtypeStruct((N_IN_SEG - 1, b, s, d), BF16),
        scratch_shapes=[pltpu.VMEM((s, d), BF16), pltpu.VMEM((s, d), BF16)],
        compiler_params=_cparams("arbitrary", "arbitrary"),
        name="in_proj",
    )(x3, g, w_in, gm_g, w_s, b_st)


def _split3(a):
    hi = a.astype(BF16)
    r1 = a - hi.astype(F32)
    mid = r1.astype(BF16)
    lo = (r1 - mid.astype(F32)).astype(BF16)
    return hi, mid, lo


def _log_sigmoid(x):
    return jnp.minimum(x, 0.0) - jnp.log(1.0 + jnp.exp(-jnp.abs(x)))


def _mlstm_kernel(xm_ref, so_ref, cw_ref, cb_ref, wq_ref, wk_ref, wv_ref, wg_ref, wgt_ref,
                  bg_ref, bgt_ref, hg_ref, sk_ref, out_ref,
                  pad_s, xc_s, gc_s, gr_s, q_s, k_s, v_s, hf_s, ct_s):
    s = xm_ref.shape[2]
    d = xm_ref.shape[3]
    dh = d // ML_HEADS
    n_chunks = s // CHUNK
    nh = ML_HEADS

    ri = lax.broadcasted_iota(I32, (CHUNK, CHUNK), 0)
    ci = lax.broadcasted_iota(I32, (CHUNK, CHUNK), 1)
    tril = ri >= ci
    triu = ri <= ci
    tril_b = jnp.where(tril, 1.0, 0.0).astype(BF16)
    triu_b = jnp.where(triu, 1.0, 0.0).astype(BF16)

    pad_s[0:CONV_PAD, :] = jnp.zeros((CONV_PAD, d), F32)
    pad_s[s + CONV_PAD:s + 2 * CONV_PAD, :] = jnp.zeros((CONV_PAD, d), F32)

    def fill(c, carry):
        pad_s[pl.ds(pl.multiple_of(c * CHUNK, CHUNK) + CONV_PAD, CHUNK), :] = (
            xm_ref[0, 0, _rows(c), :].astype(F32))
        return carry
    lax.fori_loop(0, n_chunks, fill, 0)

    lane = lax.broadcasted_iota(I32, (s, CHUNK), 1)
    v_s[:, dh:dh + CHUNK] = jnp.where(lane == 0, 1.0, 0.0).astype(BF16)

    def conv_gates(c, carry):
        base = pl.multiple_of(c * CHUNK, CHUNK)
        gc = jnp.zeros((CHUNK, 4 * nh), F32) + bg_ref[...]
        gr = jnp.zeros((4 * nh, CHUNK), F32) + bgt_ref[...]
        for hh in range(nh):
            cs = slice(hh * dh, (hh + 1) * dh)
            acc = jnp.zeros((CHUNK, dh), F32) + cb_ref[:, cs]
            win = pad_s[pl.ds(base, CHUNK + 2 * CONV_PAD), cs]
            for t in range(ML_CONV):
                off = CONV_PAD - ML_CONV // 2 + t
                acc = acc + win[off:off + CHUNK, :] * cw_ref[t:t + 1, cs]
            xc = (acc * jax.nn.sigmoid(acc)).astype(BF16)
            xc_s[_rows(c), cs] = xc
            gc = gc + _dot(xc, wg_ref[cs, :])
            gr = gr + _dot_nt(wgt_ref[:, cs], xc)
        lf_c = _log_sigmoid(gc[:, nh:2 * nh])
        lb_c = _log_sigmoid(gc[:, 3 * nh:4 * nh])
        lf_r = _log_sigmoid(gr[nh:2 * nh, :])
        lb_r = _log_sigmoid(gr[3 * nh:4 * nh, :])
        bcf = sum(_dot(tril_b, p) for p in _split3(lf_c))
        bcb = sum(_dot(triu_b, p) for p in _split3(lb_c))
        brf = sum(_dot(p, triu_b) for p in _split3(lf_r))
        brb = sum(_dot(p, tril_b) for p in _split3(lb_r))
        gc_s[_rows(c), 0:nh] = gc[:, 0:nh]
        gc_s[_rows(c), nh:2 * nh] = bcf
        gc_s[_rows(c), 2 * nh:3 * nh] = gc[:, 2 * nh:3 * nh]
        gc_s[_rows(c), 3 * nh:4 * nh] = bcb
        gr_s[0:nh, _rows(c)] = gr[0:nh, :]
        gr_s[nh:2 * nh, _rows(c)] = brf
        gr_s[2 * nh:3 * nh, _rows(c)] = gr[2 * nh:3 * nh, :]
        gr_s[3 * nh:4 * nh, _rows(c)] = brb
        return carry
    lax.fori_loop(0, n_chunks, conv_gates, 0)

    for hh in range(nh):
        cs = slice(hh * dh, (hh + 1) * dh)

        def qkv(c, carry):
            xc = xc_s[_rows(c), cs]
            q_s[_rows(c), :] = _dot(xc, wq_ref[hh]).astype(BF16)
            k_s[_rows(c), :] = (_dot(xc, wk_ref[hh]) * (dh ** -0.5)).astype(BF16)
            v_s[_rows(c), 0:dh] = _dot(xm_ref[0, 0, _rows(c), cs], wv_ref[hh]).astype(BF16)
            return carry
        lax.fori_loop(0, n_chunks, qkv, 0)

        def step(c, m, i_col, b_col, mask, last_row):
            r = _rows(c)
            qc, kc, vx = q_s[r, :], k_s[r, :], v_s[r, :]
            bc = gc_s[r, b_col:b_col + 1]
            ic = gc_s[r, i_col:i_col + 1]
            br = gr_s[b_col:b_col + 1, r]
            ir = gr_s[i_col:i_col + 1, r]
            dm = jnp.where(mask, bc - br + ir, NEG)
            inter = bc + m
            m_t = jnp.maximum(inter, jnp.max(dm, axis=-1, keepdims=True))
            w_intra = jnp.exp(dm - m_t)
            w_inter = jnp.exp(inter - m_t)
            sc = (_dot_nt(qc, kc) * w_intra).astype(BF16)
            res = _dot(sc, vx) + w_inter * _dot(qc, ct_s[...].astype(BF16))
            num = res[:, 0:dh]
            den = res[:, dh:dh + 1]
            h = num / jnp.maximum(jnp.abs(den), jnp.exp(-m_t))
            b_l = bc[last_row:last_row + 1, :]
            g_col = b_l - bc + ic
            m_new = jnp.maximum(b_l + m, jnp.max(g_col, axis=0, keepdims=True))
            w_s = jnp.exp(g_col - m_new)
            w_c = jnp.exp(b_l + m - m_new)
            ct_s[...] = w_c * ct_s[...] + _dot_tn(kc, (w_s * vx.astype(F32)).astype(BF16))
            return h, m_new

        ct_s[...] = jnp.zeros(ct_s.shape, F32)

        def fwd(c, m):
            h, m_new = step(c, m, hh, nh + hh, tril, CHUNK - 1)
            hf_s[_rows(c), :] = h
            return m_new
        lax.fori_loop(0, n_chunks, fwd, jnp.zeros((1, 1), F32))

        ct_s[...] = jnp.zeros(ct_s.shape, F32)

        def bwd(i, m):
            c = n_chunks - 1 - i
            h, m_new = step(c, m, 2 * nh + hh, 3 * nh + hh, triu, 0)
            h = h + hf_s[_rows(c), :]
            y = _rms(h, hg_ref[:, cs])
            xc = xc_s[_rows(c), cs].astype(F32)
            so = so_ref[0, 0, _rows(c), cs].astype(F32)
            out_ref[0, _rows(c), cs] = (so * (y + sk_ref[:, cs] * xc)).astype(BF16)
            return m_new
        lax.fori_loop(0, n_chunks, bwd, jnp.zeros((1, 1), F32))


def _mlstm(z_all, cw, cb, wq, wk, wv, wg, wgt, bg, bgt, hg, sk):
    _, b, s, d = z_all.shape
    dh = d // ML_HEADS
    ng = 4 * ML_HEADS

    def full(a):
        return pl.BlockSpec(a.shape, lambda i, _n=a.ndim: (0,) * _n)

    return pl.pallas_call(
        _mlstm_kernel,
        grid=(b,),
        in_specs=[
            pl.BlockSpec((1, 1, s, d), lambda i: (1, i, 0, 0)),
            pl.BlockSpec((1, 1, s, d), lambda i: (2, i, 0, 0)),
            full(cw), full(cb), full(wq), full(wk), full(wv), full(wg), full(wgt),
            full(bg), full(bgt), full(hg), full(sk),
        ],
        out_specs=pl.BlockSpec((1, s, d), lambda i: (i, 0, 0)),
        out_shape=jax.ShapeDtypeStruct((b, s, d), BF16),
        scratch_shapes=[
            pltpu.VMEM((s + 2 * CONV_PAD, d), F32),
            pltpu.VMEM((s, d), BF16),
            pltpu.VMEM((s, ng), F32),
            pltpu.VMEM((ng, s), F32),
            pltpu.VMEM((s, dh), BF16),
            pltpu.VMEM((s, dh), BF16),
            pltpu.VMEM((s, dh + CHUNK), BF16),
            pltpu.VMEM((s, dh), F32),
            pltpu.VMEM((dh, dh + CHUNK), F32),
        ],
        compiler_params=_cparams("arbitrary"),
        name="mlstm",
    )(z_all, z_all, cw, cb, wq, wk, wv, wg, wgt, bg, bgt, hg, sk)


def _merge_kernel(x_ref, ya_ref, yb_ref, ga_ref, gb_ref, wa_ref, wb_ref, wo_ref, out_ref):
    merged = (ga_ref[0].astype(F32) * _dot(ya_ref[0], wa_ref[...])
              + gb_ref[0].astype(F32) * _dot(yb_ref[...], wb_ref[...]))
    out_ref[...] = x_ref[...] + _dot(merged.astype(BF16), wo_ref[...])


def _merge(x2, z_all, yb2, wa, wb, wo, tm):
    t, d = x2.shape
    z3 = z_all.reshape(z_all.shape[0], t, d)

    def slab(k):
        return pl.BlockSpec((1, tm, d), lambda i, _k=k: (_k, i, 0))

    row = pl.BlockSpec((tm, d), lambda i: (i, 0))
    wspec = pl.BlockSpec((d, d), lambda i: (0, 0))
    return pl.pallas_call(
        _merge_kernel,
        grid=(t // tm,),
        in_specs=[row, slab(0), row, slab(3), slab(4), wspec, wspec, wspec],
        out_specs=row,
        out_shape=jax.ShapeDtypeStruct((t, d), F32),
        compiler_params=_cparams("arbitrary"),
        name="merge",
    )(x2, z3, yb2, z3, z3, wa, wb, wo)


def _peer_route_kernel(x_ref, g_ref, wq_ref, keys_ref, xpk_ref, idx_ref, gw_ref,
                       q_s, sv_s, si_s, cv_s, ce_s, it_s, gt_s):
    tm = x_ref.shape[0]
    dw = x_ref.shape[1] // 2
    k = PEER_TOPK
    big = 1e9

    xb = _rms(x_ref[...], g_ref[...]).astype(BF16)
    bits = lax.bitcast_convert_type(xb.astype(F32), I32)
    xpk_ref[...] = lax.shift_right_logical(bits[:, :dw], jnp.full((tm, dw), 16, I32)) | bits[:, dw:]
    q_s[...] = _dot(xb, wq_ref[...]).astype(BF16)

    def sub_topk(hp, carry):
        qh = q_s[:, pl.ds(pl.multiple_of(hp * CHUNK, CHUNK), CHUNK)]
        sc = _dot_nt(keys_ref[hp], qh)
        rowf = lax.broadcasted_iota(I32, sc.shape, 0).astype(F32)
        for j in range(k):
            m = jnp.max(sc, axis=0, keepdims=True)
            i = jnp.min(jnp.where(sc == m, rowf, big), axis=0, keepdims=True)
            sv_s[hp, j:j + 1, :] = m
            si_s[hp, j:j + 1, :] = i
            sc = jnp.where(rowf == i, -jnp.inf, sc)
        return carry
    lax.fori_loop(0, 2 * PEER_HEADS, sub_topk, 0)

    def head_topk(h, carry):
        sv0, sv1 = sv_s[2 * h], sv_s[2 * h + 1]
        si0, si1 = si_s[2 * h], si_s[2 * h + 1]
        cand = jnp.concatenate([sv0[a:a + 1, :] + sv1 for a in range(k)], axis=0)
        eid = jnp.concatenate([si0[a:a + 1, :] * float(CHUNK) + si1 for a in range(k)], axis=0)
        rowf = lax.broadcasted_iota(I32, cand.shape, 0).astype(F32)
        for j in range(k):
            m = jnp.max(cand, axis=0, keepdims=True)
            i = jnp.min(jnp.where(cand == m, rowf, big), axis=0, keepdims=True)
            hit = rowf == i
            cv_s[j:j + 1, :] = m
            ce_s[j:j + 1, :] = jnp.sum(jnp.where(hit, eid, 0.0), axis=0, keepdims=True)
            cand = jnp.where(hit, -jnp.inf, cand)
        cv = cv_s[...]
        e = jnp.exp(cv - cv[0:1, :])
        rows = pl.ds(pl.multiple_of(h * k, k), k)
        gt_s[rows, :] = e / jnp.sum(e, axis=0, keepdims=True)
        it_s[rows, :] = ce_s[...]
        return carry
    lax.fori_loop(0, PEER_HEADS, head_topk, 0)

    idx_ref[...] = it_s[...].T.astype(I32)
    gw_ref[...] = gt_s[...].T


def _peer_route(x2, g, wq, keys, tm):
    t, d = x2.shape
    qd = wq.shape[1]
    nsel = PEER_HEADS * PEER_TOPK
    row = pl.BlockSpec((tm, d), lambda i: (i, 0))
    sel = pl.BlockSpec((tm, nsel), lambda i: (i, 0))
    return pl.pallas_call(
        _peer_route_kernel,
        grid=(t // tm,),
        in_specs=[
            row,
            pl.BlockSpec((1, d), lambda i: (0, 0)),
            pl.BlockSpec((d, qd), lambda i: (0, 0)),
            pl.BlockSpec(keys.shape, lambda i: (0, 0, 0)),
        ],
        out_specs=[pl.BlockSpec((tm, d // 2), lambda i: (i, 0)), sel, sel],
        out_shape=[
            jax.ShapeDtypeStruct((t, d // 2), I32),
            jax.ShapeDtypeStruct((t, nsel), I32),
            jax.ShapeDtypeStruct((t, nsel), F32),
        ],
        scratch_shapes=[
            pltpu.VMEM((tm, qd), BF16),
            pltpu.VMEM((2 * PEER_HEADS, PEER_TOPK, tm), F32),
            pltpu.VMEM((2 * PEER_HEADS, PEER_TOPK, tm), F32),
            pltpu.VMEM((PEER_TOPK, tm), F32),
            pltpu.VMEM((PEER_TOPK, tm), F32),
            pltpu.VMEM((nsel, tm), F32),
            pltpu.VMEM((nsel, tm), F32),
        ],
        compiler_params=_cparams("arbitrary"),
        name="peer_route",
    )(x2, g, wq, keys)


SC_LANES = 16
SC_TOK_BLOCK = 8
SC_RING = 8
SC_AHEAD = 6


def _pack_rows(tab):
    d = tab.shape[1]
    bits = lax.bitcast_convert_type(tab.astype(BF16), jnp.uint16).astype(jnp.uint32)
    return lax.bitcast_convert_type(bits[:, :d // 2] | (bits[:, d // 2:] << 16), I32)


def _unpack_words(w):
    lo = lax.bitcast_convert_type(lax.shift_left(w, jnp.full(w.shape, 16, I32)), F32)
    hi = lax.bitcast_convert_type(w & jnp.full(w.shape, -65536, I32), F32)
    return lo, hi


def _mul2_packed(a0, b0, a1, b1):
    p = (plsc.bitcast(a0, BF16) * plsc.bitcast(b0, BF16)
         + plsc.bitcast(a1, BF16) * plsc.bitcast(b1, BF16))
    return _unpack_words(plsc.bitcast(p, I32))


def _round_to_packed_pair(x):
    bits = lax.bitcast_convert_type(x, I32)
    odd = lax.shift_right_logical(bits, jnp.full(x.shape, 16, I32)) & jnp.full(x.shape, 1, I32)
    hi = (bits + jnp.full(x.shape, 0x7FFF, I32) + odd) & jnp.full(x.shape, -65536, I32)
    return hi | lax.shift_right_logical(hi, jnp.full(x.shape, 16, I32))


def _peer_experts(u_pk, v_pk, idx, gw, xn_pk, xres):
    t, d = xres.shape
    k = PEER_TOPK
    nh = PEER_HEADS
    dw = d // 2
    n_steps = 2 * nh
    assert k == SC_LANES and n_steps % SC_RING == 0 and SC_AHEAD < SC_RING and dw % (2 * SC_LANES) == 0
    info = plsc.get_sparse_core_info()
    nc, ns = info.num_cores, info.num_subcores
    nw = nc * ns
    tb = SC_TOK_BLOCK
    assert t % (nw * tb) == 0
    tpw = t // nw
    n_chunks = dw // SC_LANES
    mesh = plsc.VectorSubcoreMesh(core_axis_name="c", subcore_axis_name="s")

    @functools.partial(
        pl.kernel,
        mesh=mesh,
        out_type=jax.ShapeDtypeStruct((t, d), F32),
        scratch_types=[
            pltpu.VMEM((tb, nh * k), I32),
            pltpu.VMEM((tb, nh * k), F32),
            pltpu.VMEM((tb, dw), I32),
            pltpu.VMEM((tb, d), F32),
            pltpu.VMEM((SC_RING, k, dw), I32),
            pltpu.VMEM((k, SC_LANES), F32),
            pltpu.VMEM((nh, k), F32),
            pltpu.VMEM((nh, k), I32),
            pltpu.SemaphoreType.DMA((SC_RING,)),
        ],
        compiler_params=pltpu.CompilerParams(needs_layout_passes=False),
        name="peer_experts",
    )
    def run(u_hbm, v_hbm, idx_hbm, gw_hbm, xn_hbm, xres_hbm, out_hbm,
            idx_v, gw_v, xn_v, out_v, ring, red_v, act_v, w_v, sems):
        wid = lax.axis_index("s") * nc + lax.axis_index("c")
        tok0 = wid * tpw
        lane = lax.iota(I32, SC_LANES)

        def gather(tok, step):
            tab = u_hbm if step < nh else v_hbm
            slot = step % SC_RING
            return pltpu.make_async_copy(
                tab.at[idx_v.at[tok, pl.ds((step % nh) * k, k)]], ring.at[slot], sems.at[slot])

        @pl.loop(0, tpw // tb)
        def _(blk):
            base = tok0 + blk * tb
            pltpu.sync_copy(idx_hbm.at[pl.ds(base, tb)], idx_v)
            pltpu.sync_copy(gw_hbm.at[pl.ds(base, tb)], gw_v)
            pltpu.sync_copy(xn_hbm.at[pl.ds(base, tb)], xn_v)
            pltpu.sync_copy(xres_hbm.at[pl.ds(base, tb)], out_v)
            for step in range(SC_AHEAD):
                gather(0, step).start()

            @pl.loop(0, tb)
            def _(tok):
                for step in range(n_steps):
                    h = step % nh
                    slot = step % SC_RING
                    ahead = step + SC_AHEAD
                    if ahead < n_steps:
                        gather(tok, ahead).start()
                    else:
                        @pl.when(tok + 1 < tb)
                        def _():
                            gather(tok + 1, ahead - n_steps).start()

                    if step == nh:
                        for hh in range(nh):
                            act = act_v[hh, :]
                            z = GELU_C * (act + 0.044715 * act * act * act)
                            tanh_z = 1.0 - 2.0 / (jnp.exp(2.0 * z) + 1.0)
                            w_v[hh, :] = _round_to_packed_pair(
                                gw_v[tok, pl.ds(hh * k, k)] * (0.5 * act * (1.0 + tanh_z)))

                    gather(tok, step).wait()

                    if step < nh:
                        def dot_rows(c, accs):
                            w0 = pl.multiple_of(c * 2 * SC_LANES, 2 * SC_LANES)
                            w1 = w0 + SC_LANES
                            x0 = xn_v[tok, pl.ds(w0, SC_LANES)]
                            x1 = xn_v[tok, pl.ds(w1, SC_LANES)]
                            out = []
                            for r in range(k):
                                lo, hi = _mul2_packed(ring[slot, r, pl.ds(w0, SC_LANES)], x0,
                                                      ring[slot, r, pl.ds(w1, SC_LANES)], x1)
                                out.append(accs[r] + (lo + hi))
                            return tuple(out)
                        accs = lax.fori_loop(
                            0, n_chunks // 2, dot_rows,
                            tuple(jnp.zeros((SC_LANES,), F32) for _ in range(k)))
                        for r in range(k):
                            red_v[r, :] = accs[r]
                        cols = [plsc.load_gather(red_v, [lane, jnp.full((SC_LANES,), jj, I32)])
                                for jj in range(SC_LANES)]
                        while len(cols) > 1:
                            cols = [cols[i] + cols[i + 1] for i in range(0, len(cols), 2)]
                        act_v[h, :] = cols[0]
                    else:
                        wv = w_v[h, :]
                        ws = [wv.at[jnp.full((SC_LANES,), r, I32)].get(mode="promise_in_bounds")
                              for r in range(k)]

                        @plsc.parallel_loop(0, n_chunks)
                        def _(c):
                            wo = pl.multiple_of(c * SC_LANES, SC_LANES)
                            los, his = [], []
                            for r in range(0, k, 2):
                                lo, hi = _mul2_packed(ring[slot, r, pl.ds(wo, SC_LANES)], ws[r],
                                                      ring[slot, r + 1, pl.ds(wo, SC_LANES)], ws[r + 1])
                                los.append(lo)
                                his.append(hi)
                            while len(los) > 1:
                                los = [los[i] + los[i + 1] for i in range(0, len(los), 2)]
                                his = [his[i] + his[i + 1] for i in range(0, len(his), 2)]
                            plsc.addupdate(out_v.at[tok, pl.ds(wo, SC_LANES)], los[0])
                            plsc.addupdate(out_v.at[tok, pl.ds(dw + wo, SC_LANES)], his[0])

            pltpu.sync_copy(out_v, out_hbm.at[pl.ds(base, tb)])

    return run(u_pk, v_pk, idx, gw, xn_pk, xres)


def _final_norm_kernel(x_ref, g_ref, out_ref):
    out_ref[...] = _rms(x_ref[...], g_ref[...])


def _final_norm(x2, g, tm):
    t, d = x2.shape
    row = pl.BlockSpec((tm, d), lambda i: (i, 0))
    return pl.pallas_call(
        _final_norm_kernel,
        grid=(t // tm,),
        in_specs=[row, pl.BlockSpec((1, d), lambda i: (0, 0))],
        out_specs=row,
        out_shape=jax.ShapeDtypeStruct((t, d), F32),
        compiler_params=_cparams("arbitrary"),
        name="final_norm",
    )(x2, g)


def _row_tile(t, want):
    tm = min(want, t)
    assert t % tm == 0
    return tm


def kernel(x, norm_mix_g, w_in, gm_norm_g, gm_w_s, gm_b_s, ml_conv_w, ml_conv_b, ml_w_q, ml_w_k, ml_w_v, ml_w_gate, ml_b_gate, ml_head_g, ml_skip, w_branch_a, w_branch_b, w_out, norm_ffn_g, peer_w_query, peer_sub_keys, peer_u, peer_v, final_g):
    b, s, d = x.shape
    depth = w_in.shape[0]
    assert s % CHUNK == 0 and d % (ML_HEADS * CHUNK) == 0 and w_in.shape[2] == N_IN_SEG * d
    n_groups = BATCH_GROUPS if b % BATCH_GROUPS == 0 else 1
    bg = b // n_groups
    t = bg * s
    tm_merge = _row_tile(t, 512)
    tm_route = _row_tile(t, 256)
    parts = [x[i * bg:(i + 1) * bg].reshape(t, d) for i in range(n_groups)]
    for l in range(depth):
        w_in_b = w_in[l].astype(BF16)
        w_s_b = gm_w_s[l].astype(BF16)
        wq_b, wk_b, wv_b = (ml_w_q[l].astype(BF16), ml_w_k[l].astype(BF16), ml_w_v[l].astype(BF16))
        wg = ml_w_gate[l].astype(BF16)
        wa_b, wb_b, wo_b = (w_branch_a[l].astype(BF16), w_branch_b[l].astype(BF16),
                            w_out[l].astype(BF16))
        wquery_b = peer_w_query[l].astype(BF16)
        keys = peer_sub_keys[l].reshape(2 * PEER_HEADS, CHUNK, -1).astype(BF16)
        u_pk, v_pk = _pack_rows(peer_u[l]), _pack_rows(peer_v[l])
        for p in range(n_groups):
            x2 = parts[p]
            z_all = _in_proj(
                x2.reshape(bg, s, d), norm_mix_g[l].reshape(1, d), w_in_b,
                gm_norm_g[l].reshape(1, d), w_s_b, gm_b_s[l].T)
            yb = _mlstm(
                z_all, ml_conv_w[l], ml_conv_b[l].reshape(1, d), wq_b, wk_b, wv_b,
                wg, wg.T, ml_b_gate[l].reshape(1, -1), ml_b_gate[l].reshape(-1, 1),
                ml_head_g[l].reshape(1, d), ml_skip[l].reshape(1, d))
            x2 = _merge(x2, z_all, yb.reshape(t, d), wa_b, wb_b, wo_b, tm_merge)
            xn_pk, idx, gw = _peer_route(x2, norm_ffn_g[l].reshape(1, d), wquery_b, keys, tm_route)
            parts[p] = _peer_experts(u_pk, v_pk, idx, gw, xn_pk, x2)
    outs = [_final_norm(p, final_g.reshape(1, d), tm_merge) for p in parts]
    return jnp.concatenate(outs, axis=0).reshape(b, s, d)
```

```python
import functools

import jax
import jax.numpy as jnp
from jax import lax
from jax.experimental import pallas as pl
from jax.experimental.pallas import tpu as pltpu
from jax.experimental.pallas import tpu_sc as plsc

F32 = jnp.float32
BF16 = jnp.bfloat16
I32 = jnp.int32

EPS = 1e-6
NEG = -1e30
CHUNK = 128
GM_GROUPS = 8
ML_HEADS = 4
ML_CONV = 5
CONV_PAD = 8
PEER_HEADS = 8
PEER_TOPK = 16
N_IN_SEG = 6
BATCH_GROUPS = 4
GELU_C = 0.7978845608028654

VMEM_LIMIT = 56 * 1024 * 1024


def _cparams(*sem):
    return pltpu.CompilerParams(dimension_semantics=sem, vmem_limit_bytes=VMEM_LIMIT)


def _rows(c, n=CHUNK):
    return pl.ds(pl.multiple_of(c * n, n), n)


def _rms(x, g):
    return x * lax.rsqrt(jnp.mean(x * x, axis=-1, keepdims=True) + EPS) * g


def _dot(a, b):
    return jnp.dot(a, b, preferred_element_type=F32)


def _dot_nt(a, b):
    return lax.dot_general(a, b, (((1,), (1,)), ((), ())), preferred_element_type=F32)


def _dot_tn(a, b):
    return lax.dot_general(a, b, (((0,), (0,)), ((), ())), preferred_element_type=F32)


def _in_proj_kernel(x_ref, g_ref, w_ref, gmg_ref, ws_ref, bst_ref, out_ref, hn_s, gu_s):
    j = pl.program_id(1)
    n_chunks = x_ref.shape[1] // CHUNK
    d = x_ref.shape[2]

    def z_of(c):
        return _dot(hn_s[_rows(c), :], w_ref[...])

    def for_chunks(fn):
        def body(c, carry):
            fn(c)
            return carry
        lax.fori_loop(0, n_chunks, body, 0)

    @pl.when(j == 0)
    def _():
        def norm(c):
            hn_s[_rows(c), :] = _rms(x_ref[0, _rows(c), :], g_ref[...]).astype(BF16)
        for_chunks(norm)

        def seg_u(c):
            gu_s[_rows(c), :] = jax.nn.gelu(z_of(c)).astype(BF16)
        for_chunks(seg_u)

    @pl.when(j == 1)
    def _():
        def seg_v(c):
            vn = _rms(jax.nn.gelu(z_of(c)), gmg_ref[...]).astype(BF16)
            gu = gu_s[_rows(c), :]
            for g in range(GM_GROUPS):
                cs = slice(g * (d // GM_GROUPS), (g + 1) * (d // GM_GROUPS))
                mixed = _dot(ws_ref[g], vn[:, cs]) + bst_ref[:, g:g + 1]
                out_ref[0, 0, _rows(c), cs] = (gu[:, cs].astype(F32) * mixed).astype(BF16)
        for_chunks(seg_v)

    @pl.when(j == 2)
    def _():
        def seg_xm(c):
            out_ref[0, 0, _rows(c), :] = z_of(c).astype(BF16)
        for_chunks(seg_xm)

    @pl.when(j >= 3)
    def _():
        def seg_gate(c):
            out_ref[0, 0, _rows(c), :] = jax.nn.sigmoid(z_of(c)).astype(BF16)
        for_chunks(seg_gate)


def _in_proj(x3, g, w_in, gm_g, w_s, b_st):
    b, s, d = x3.shape
    return pl.pallas_call(
        _in_proj_kernel,
        grid=(b, N_IN_SEG),
        in_specs=[
            pl.BlockSpec((1, s, d), lambda i, j: (i, 0, 0)),
            pl.BlockSpec((1, d), lambda i, j: (0, 0)),
            pl.BlockSpec((d, d), lambda i, j: (0, j)),
            pl.BlockSpec((1, d), lambda i, j: (0, 0)),
            pl.BlockSpec((GM_GROUPS, CHUNK, CHUNK), lambda i, j: (0, 0, 0)),
            pl.BlockSpec((CHUNK, GM_GROUPS), lambda i, j: (0, 0)),
        ],
        out_specs=pl.BlockSpec((1, 1, s, d), lambda i, j: (jnp.maximum(j - 1, 0), i, 0, 0)),
        out_shape=jax.ShapeDtypeStruct((N_IN_SEG - 1, b, s, d), BF16),
        scratch_shapes=[pltpu.VMEM((s, d), BF16), pltpu.VMEM((s, d), BF16)],
        compiler_params=_cparams("arbitrary", "arbitrary"),
        name="in_proj",
    )(x3, g, w_in, gm_g, w_s, b_st)


def _split3(a):
    hi = a.astype(BF16)
    r1 = a - hi.astype(F32)
    mid = r1.astype(BF16)
    lo = (r1 - mid.astype(F32)).astype(BF16)
    return hi, mid, lo


def _log_sigmoid(x):
    return jnp.minimum(x, 0.0) - jnp.log(1.0 + jnp.exp(-jnp.abs(x)))


def _mlstm_kernel(xm_ref, so_ref, cw_ref, cb_ref, wq_ref, wk_ref, wv_ref, wg_ref, wgt_ref,
                  bg_ref, bgt_ref, hg_ref, sk_ref, out_ref,
                  pad_s, xc_s, gc_s, gr_s, q_s, k_s, v_s, hf_s, ct_s):
    s = xm_ref.shape[2]
    d = xm_ref.shape[3]
    dh = d // ML_HEADS
    n_chunks = s // CHUNK
    nh = ML_HEADS

    ri = lax.broadcasted_iota(I32, (CHUNK, CHUNK), 0)
    ci = lax.broadcasted_iota(I32, (CHUNK, CHUNK), 1)
    tril = ri >= ci
    triu = ri <= ci
    tril_b = jnp.where(tril, 1.0, 0.0).astype(BF16)
    triu_b = jnp.where(triu, 1.0, 0.0).astype(BF16)

    pad_s[0:CONV_PAD, :] = jnp.zeros((CONV_PAD, d), F32)
    pad_s[s + CONV_PAD:s + 2 * CONV_PAD, :] = jnp.zeros((CONV_PAD, d), F32)

    def fill(c, carry):
        pad_s[pl.ds(pl.multiple_of(c * CHUNK, CHUNK) + CONV_PAD, CHUNK), :] = (
            xm_ref[0, 0, _rows(c), :].astype(F32))
        return carry
    lax.fori_loop(0, n_chunks, fill, 0)

    lane = lax.broadcasted_iota(I32, (s, CHUNK), 1)
    v_s[:, dh:dh + CHUNK] = jnp.where(lane == 0, 1.0, 0.0).astype(BF16)

    def conv_gates(c, carry):
        base = pl.multiple_of(c * CHUNK, CHUNK)
        gc = jnp.zeros((CHUNK, 4 * nh), F32) + bg_ref[...]
        gr = jnp.zeros((4 * nh, CHUNK), F32) + bgt_ref[...]
        for hh in range(nh):
            cs = slice(hh * dh, (hh + 1) * dh)
            acc = jnp.zeros((CHUNK, dh), F32) + cb_ref[:, cs]
            win = pad_s[pl.ds(base, CHUNK + 2 * CONV_PAD), cs]
            for t in range(ML_CONV):
                off = CONV_PAD - ML_CONV // 2 + t
                acc = acc + win[off:off + CHUNK, :] * cw_ref[t:t + 1, cs]
            xc = (acc * jax.nn.sigmoid(acc)).astype(BF16)
            xc_s[_rows(c), cs] = xc
            gc = gc + _dot(xc, wg_ref[cs, :])
            gr = gr + _dot_nt(wgt_ref[:, cs], xc)
        lf_c = _log_sigmoid(gc[:, nh:2 * nh])
        lb_c = _log_sigmoid(gc[:, 3 * nh:4 * nh])
        lf_r = _log_sigmoid(gr[nh:2 * nh, :])
        lb_r = _log_sigmoid(gr[3 * nh:4 * nh, :])
        bcf = sum(_dot(tril_b, p) for p in _split3(lf_c))
        bcb = sum(_dot(triu_b, p) for p in _split3(lb_c))
        brf = sum(_dot(p, triu_b) for p in _split3(lf_r))
        brb = sum(_dot(p, tril_b) for p in _split3(lb_r))
        gc_s[_rows(c), 0:nh] = gc[:, 0:nh]
        gc_s[_rows(c), nh:2 * nh] = bcf
        gc_s[_rows(c), 2 * nh:3 * nh] = gc[:, 2 * nh:3 * nh]
        gc_s[_rows(c), 3 * nh:4 * nh] = bcb
        gr_s[0:nh, _rows(c)] = gr[0:nh, :]
        gr_s[nh:2 * nh, _rows(c)] = brf
        gr_s[2 * nh:3 * nh, _rows(c)] = gr[2 * nh:3 * nh, :]
        gr_s[3 * nh:4 * nh, _rows(c)] = brb
        return carry
    lax.fori_loop(0, n_chunks, conv_gates, 0)

    for hh in range(nh):
        cs = slice(hh * dh, (hh + 1) * dh)

        def qkv(c, carry):
            xc = xc_s[_rows(c), cs]
            q_s[_rows(c), :] = _dot(xc, wq_ref[hh]).astype(BF16)
            k_s[_rows(c), :] = (_dot(xc, wk_ref[hh]) * (dh ** -0.5)).astype(BF16)
            v_s[_rows(c), 0:dh] = _dot(xm_ref[0, 0, _rows(c), cs], wv_ref[hh]).astype(BF16)
            return carry
        lax.fori_loop(0, n_chunks, qkv, 0)

        def step(c, m, i_col, b_col, mask, last_row):
            r = _rows(c)
            qc, kc, vx = q_s[r, :], k_s[r, :], v_s[r, :]
            bc = gc_s[r, b_col:b_col + 1]
            ic = gc_s[r, i_col:i_col + 1]
            br = gr_s[b_col:b_col + 1, r]
            ir = gr_s[i_col:i_col + 1, r]
            dm = jnp.where(mask, bc - br + ir, NEG)
            inter = bc + m
            m_t = jnp.maximum(inter, jnp.max(dm, axis=-1, keepdims=True))
            w_intra = jnp.exp(dm - m_t)
            w_inter = jnp.exp(inter - m_t)
            sc = (_dot_nt(qc, kc) * w_intra).astype(BF16)
            res = _dot(sc, vx) + w_inter * _dot(qc, ct_s[...].astype(BF16))
            num = res[:, 0:dh]
            den = res[:, dh:dh + 1]
            h = num / jnp.maximum(jnp.abs(den), jnp.exp(-m_t))
            b_l = bc[last_row:last_row + 1, :]
            g_col = b_l - bc + ic
            m_new = jnp.maximum(b_l + m, jnp.max(g_col, axis=0, keepdims=True))
            w_s = jnp.exp(g_col - m_new)
            w_c = jnp.exp(b_l + m - m_new)
            ct_s[...] = w_c * ct_s[...] + _dot_tn(kc, (w_s * vx.astype(F32)).astype(BF16))
            return h, m_new

        ct_s[...] = jnp.zeros(ct_s.shape, F32)

        def fwd(c, m):
            h, m_new = step(c, m, hh, nh + hh, tril, CHUNK - 1)
            hf_s[_rows(c), :] = h
            return m_new
        lax.fori_loop(0, n_chunks, fwd, jnp.zeros((1, 1), F32))

        ct_s[...] = jnp.zeros(ct_s.shape, F32)

        def bwd(i, m):
            c = n_chunks - 1 - i
            h, m_new = step(c, m, 2 * nh + hh, 3 * nh + hh, triu, 0)
            h = h + hf_s[_rows(c), :]
            y = _rms(h, hg_ref[:, cs])
            xc = xc_s[_rows(c), cs].astype(F32)
            so = so_ref[0, 0, _rows(c), cs].astype(F32)
            out_ref[0, _rows(c), cs] = (so * (y + sk_ref[:, cs] * xc)).astype(BF16)
            return m_new
        lax.fori_loop(0, n_chunks, bwd, jnp.zeros((1, 1), F32))


def _mlstm(z_all, cw, cb, wq, wk, wv, wg, wgt, bg, bgt, hg, sk):
    _, b, s, d = z_all.shape
    dh = d // ML_HEADS
    ng = 4 * ML_HEADS

    def full(a):
        return pl.BlockSpec(a.shape, lambda i, _n=a.ndim: (0,) * _n)

    return pl.pallas_call(
        _mlstm_kernel,
        grid=(b,),
        in_specs=[
            pl.BlockSpec((1, 1, s, d), lambda i: (1, i, 0, 0)),
            pl.BlockSpec((1, 1, s, d), lambda i: (2, i, 0, 0)),
            full(cw), full(cb), full(wq), full(wk), full(wv), full(wg), full(wgt),
            full(bg), full(bgt), full(hg), full(sk),
        ],
        out_specs=pl.BlockSpec((1, s, d), lambda i: (i, 0, 0)),
        out_shape=jax.ShapeDtypeStruct((b, s, d), BF16),
        scratch_shapes=[
            pltpu.VMEM((s + 2 * CONV_PAD, d), F32),
            pltpu.VMEM((s, d), BF16),
            pltpu.VMEM((s, ng), F32),
            pltpu.VMEM((ng, s), F32),
            pltpu.VMEM((s, dh), BF16),
            pltpu.VMEM((s, dh), BF16),
            pltpu.VMEM((s, dh + CHUNK), BF16),
            pltpu.VMEM((s, dh), F32),
            pltpu.VMEM((dh, dh + CHUNK), F32),
        ],
        compiler_params=_cparams("arbitrary"),
        name="mlstm",
    )(z_all, z_all, cw, cb, wq, wk, wv, wg, wgt, bg, bgt, hg, sk)


def _merge_kernel(x_ref, ya_ref, yb_ref, ga_ref, gb_ref, wa_ref, wb_ref, wo_ref, out_ref):
    merged = (ga_ref[0].astype(F32) * _dot(ya_ref[0], wa_ref[...])
              + gb_ref[0].astype(F32) * _dot(yb_ref[...], wb_ref[...]))
    out_ref[...] = x_ref[...] + _dot(merged.astype(BF16), wo_ref[...])


def _merge(x2, z_all, yb2, wa, wb, wo, tm):
    t, d = x2.shape
    z3 = z_all.reshape(z_all.shape[0], t, d)

    def slab(k):
        return pl.BlockSpec((1, tm, d), lambda i, _k=k: (_k, i, 0))

    row = pl.BlockSpec((tm, d), lambda i: (i, 0))
    wspec = pl.BlockSpec((d, d), lambda i: (0, 0))
    return pl.pallas_call(
        _merge_kernel,
        grid=(t // tm,),
        in_specs=[row, slab(0), row, slab(3), slab(4), wspec, wspec, wspec],
        out_specs=row,
        out_shape=jax.ShapeDtypeStruct((t, d), F32),
        compiler_params=_cparams("arbitrary"),
        name="merge",
    )(x2, z3, yb2, z3, z3, wa, wb, wo)


def _peer_route_kernel(x_ref, g_ref, wq_ref, keys_ref, xpk_ref, idx_ref, gw_ref,
                       q_s, sv_s, si_s, cv_s, ce_s, it_s, gt_s):
    tm = x_ref.shape[0]
    dw = x_ref.shape[1] // 2
    k = PEER_TOPK
    big = 1e9

    xb = _rms(x_ref[...], g_ref[...]).astype(BF16)
    bits = lax.bitcast_convert_type(xb.astype(F32), I32)
    xpk_ref[...] = lax.shift_right_logical(bits[:, :dw], jnp.full((tm, dw), 16, I32)) | bits[:, dw:]
    q_s[...] = _dot(xb, wq_ref[...]).astype(BF16)

    def sub_topk(hp, carry):
        qh = q_s[:, pl.ds(pl.multiple_of(hp * CHUNK, CHUNK), CHUNK)]
        sc = _dot_nt(keys_ref[hp], qh)
        rowf = lax.broadcasted_iota(I32, sc.shape, 0).astype(F32)
        for j in range(k):
            m = jnp.max(sc, axis=0, keepdims=True)
            i = jnp.min(jnp.where(sc == m, rowf, big), axis=0, keepdims=True)
            sv_s[hp, j:j + 1, :] = m
            si_s[hp, j:j + 1, :] = i
            sc = jnp.where(rowf == i, -jnp.inf, sc)
        return carry
    lax.fori_loop(0, 2 * PEER_HEADS, sub_topk, 0)

    def head_topk(h, carry):
        sv0, sv1 = sv_s[2 * h], sv_s[2 * h + 1]
        si0, si1 = si_s[2 * h], si_s[2 * h + 1]
        half = k // 2
        cand = jnp.concatenate(
            [sv0[0:1, :] + sv1] + [sv0[a:a + 1, :] + sv1[0:half, :] for a in range(1, half)]
            + [sv0[half:k, :] + sv1[0:1, :]], axis=0)
        eid = jnp.concatenate(
            [si0[0:1, :] * float(CHUNK) + si1]
            + [si0[a:a + 1, :] * float(CHUNK) + si1[0:half, :] for a in range(1, half)]
            + [si0[half:k, :] * float(CHUNK) + si1[0:1, :]], axis=0)
        row = lax.broadcasted_iota(I32, cand.shape, 0)
        n_mid = k + half * (half - 1)
        flat = jnp.where(row < k, row,
                         jnp.where(row < n_mid, row + half * ((row - k) // half),
                                   k * (row - n_mid + half)))
        rowf = flat.astype(F32)
        for j in range(k):
            m = jnp.max(cand, axis=0, keepdims=True)
            i = jnp.min(jnp.where(cand == m, rowf, big), axis=0, keepdims=True)
            hit = rowf == i
            cv_s[j:j + 1, :] = m
            ce_s[j:j + 1, :] = jnp.sum(jnp.where(hit, eid, 0.0), axis=0, keepdims=True)
            cand = jnp.where(hit, -jnp.inf, cand)
        cv = cv_s[...]
        e = jnp.exp(cv - cv[0:1, :])
        rows = pl.ds(pl.multiple_of(h * k, k), k)
        gt_s[rows, :] = e / jnp.sum(e, axis=0, keepdims=True)
        it_s[rows, :] = ce_s[...]
        return carry
    lax.fori_loop(0, PEER_HEADS, head_topk, 0)

    idx_ref[...] = it_s[...].T.astype(I32)
    gw_ref[...] = gt_s[...].T


def _peer_route(x2, g, wq, keys, tm):
    t, d = x2.shape
    qd = wq.shape[1]
    nsel = PEER_HEADS * PEER_TOPK
    row = pl.BlockSpec((tm, d), lambda i: (i, 0))
    sel = pl.BlockSpec((tm, nsel), lambda i: (i, 0))
    return pl.pallas_call(
        _peer_route_kernel,
        grid=(t // tm,),
        in_specs=[
            row,
            pl.BlockSpec((1, d), lambda i: (0, 0)),
            pl.BlockSpec((d, qd), lambda i: (0, 0)),
            pl.BlockSpec(keys.shape, lambda i: (0, 0, 0)),
        ],
        out_specs=[pl.BlockSpec((tm, d // 2), lambda i: (i, 0)), sel, sel],
        out_shape=[
            jax.ShapeDtypeStruct((t, d // 2), I32),
            jax.ShapeDtypeStruct((t, nsel), I32),
            jax.ShapeDtypeStruct((t, nsel), F32),
        ],
        scratch_shapes=[
            pltpu.VMEM((tm, qd), BF16),
            pltpu.VMEM((2 * PEER_HEADS, PEER_TOPK, tm), F32),
            pltpu.VMEM((2 * PEER_HEADS, PEER_TOPK, tm), F32),
            pltpu.VMEM((PEER_TOPK, tm), F32),
            pltpu.VMEM((PEER_TOPK, tm), F32),
            pltpu.VMEM((nsel, tm), F32),
            pltpu.VMEM((nsel, tm), F32),
        ],
        compiler_params=_cparams("arbitrary"),
        name="peer_route",
    )(x2, g, wq, keys)


SC_LANES = 16
SC_TOK_BLOCK = 16
SC_RING = 8
SC_AHEAD = 6


def _pack_rows(tab):
    d = tab.shape[1]
    bits = lax.bitcast_convert_type(tab.astype(BF16), jnp.uint16).astype(jnp.uint32)
    return lax.bitcast_convert_type(bits[:, :d // 2] | (bits[:, d // 2:] << 16), I32)


def _unpack_words(w):
    lo = lax.bitcast_convert_type(lax.shift_left(w, jnp.full(w.shape, 16, I32)), F32)
    hi = lax.bitcast_convert_type(w & jnp.full(w.shape, -65536, I32), F32)
    return lo, hi


def _mul2_packed(a0, b0, a1, b1):
    p = (plsc.bitcast(a0, BF16) * plsc.bitcast(b0, BF16)
         + plsc.bitcast(a1, BF16) * plsc.bitcast(b1, BF16))
    return _unpack_words(plsc.bitcast(p, I32))


def _round_to_packed_pair(x):
    bits = lax.bitcast_convert_type(x, I32)
    odd = lax.shift_right_logical(bits, jnp.full(x.shape, 16, I32)) & jnp.full(x.shape, 1, I32)
    hi = (bits + jnp.full(x.shape, 0x7FFF, I32) + odd) & jnp.full(x.shape, -65536, I32)
    return hi | lax.shift_right_logical(hi, jnp.full(x.shape, 16, I32))


def _peer_experts(u_pk, v_pk, idx, gw, xn_pk, xres):
    t, d = xres.shape
    k = PEER_TOPK
    nh = PEER_HEADS
    dw = d // 2
    n_steps = 2 * nh
    assert k == SC_LANES and n_steps % SC_RING == 0 and SC_AHEAD < SC_RING and dw % (2 * SC_LANES) == 0
    info = plsc.get_sparse_core_info()
    nc, ns = info.num_cores, info.num_subcores
    nw = nc * ns
    tb = SC_TOK_BLOCK
    assert t % (nw * tb) == 0
    tpw = t // nw
    n_chunks = dw // SC_LANES
    mesh = plsc.VectorSubcoreMesh(core_axis_name="c", subcore_axis_name="s")

    @functools.partial(
        pl.kernel,
        mesh=mesh,
        out_type=jax.ShapeDtypeStruct((t, d), F32),
        scratch_types=[
            pltpu.VMEM((tb, nh * k), I32),
            pltpu.VMEM((tb, nh * k), F32),
            pltpu.VMEM((tb, dw), I32),
            pltpu.VMEM((tb, d), F32),
            pltpu.VMEM((SC_RING, k, dw), I32),
            pltpu.VMEM((k, SC_LANES), F32),
            pltpu.VMEM((nh, k), F32),
            pltpu.VMEM((nh, k), I32),
            pltpu.SemaphoreType.DMA((SC_RING,)),
        ],
        compiler_params=pltpu.CompilerParams(needs_layout_passes=False),
        name="peer_experts",
    )
    def run(u_hbm, v_hbm, idx_hbm, gw_hbm, xn_hbm, xres_hbm, out_hbm,
            idx_v, gw_v, xn_v, out_v, ring, red_v, act_v, w_v, sems):
        wid = lax.axis_index("s") * nc + lax.axis_index("c")
        tok0 = wid * tpw
        lane = lax.iota(I32, SC_LANES)

        def gather(tok, step):
            tab = u_hbm if step < nh else v_hbm
            slot = step % SC_RING
            return pltpu.make_async_copy(
                tab.at[idx_v.at[tok, pl.ds((step % nh) * k, k)]], ring.at[slot], sems.at[slot])

        @pl.loop(0, tpw // tb)
        def _(blk):
            base = tok0 + blk * tb
            pltpu.sync_copy(idx_hbm.at[pl.ds(base, tb)], idx_v)
            pltpu.sync_copy(gw_hbm.at[pl.ds(base, tb)], gw_v)
            pltpu.sync_copy(xn_hbm.at[pl.ds(base, tb)], xn_v)
            pltpu.sync_copy(xres_hbm.at[pl.ds(base, tb)], out_v)
            for step in range(SC_AHEAD):
                gather(0, step).start()

            @pl.loop(0, tb)
            def _(tok):
                for step in range(n_steps):
                    h = step % nh
                    slot = step % SC_RING
                    ahead = step + SC_AHEAD
                    if ahead < n_steps:
                        gather(tok, ahead).start()
                    else:
                        @pl.when(tok + 1 < tb)
                        def _():
                            gather(tok + 1, ahead - n_steps).start()

                    if step == nh:
                        for hh in range(nh):
                            act = act_v[hh, :]
                            z = GELU_C * (act + 0.044715 * act * act * act)
                            tanh_z = 1.0 - 2.0 / (jnp.exp(2.0 * z) + 1.0)
                            w_v[hh, :] = _round_to_packed_pair(
                                gw_v[tok, pl.ds(hh * k, k)] * (0.5 * act * (1.0 + tanh_z)))

                    gather(tok, step).wait()

                    if step < nh:
                        def dot_rows(c, accs):
                            w0 = pl.multiple_of(c * 2 * SC_LANES, 2 * SC_LANES)
                            w1 = w0 + SC_LANES
                            x0 = xn_v[tok, pl.ds(w0, SC_LANES)]
                            x1 = xn_v[tok, pl.ds(w1, SC_LANES)]
                            out = []
                            for r in range(k):
                                lo, hi = _mul2_packed(ring[slot, r, pl.ds(w0, SC_LANES)], x0,
                                                      ring[slot, r, pl.ds(w1, SC_LANES)], x1)
                                out.append(accs[r] + (lo + hi))
                            return tuple(out)
                        accs = lax.fori_loop(
                            0, n_chunks // 2, dot_rows,
                            tuple(jnp.zeros((SC_LANES,), F32) for _ in range(k)))
                        for r in range(k):
                            red_v[r, :] = accs[r]
                        cols = [plsc.load_gather(red_v, [lane, jnp.full((SC_LANES,), jj, I32)])
                                for jj in range(SC_LANES)]
                        while len(cols) > 1:
                            cols = [cols[i] + cols[i + 1] for i in range(0, len(cols), 2)]
                        act_v[h, :] = cols[0]
                    else:
                        wv = w_v[h, :]
                        ws = [wv.at[jnp.full((SC_LANES,), r, I32)].get(mode="promise_in_bounds")
                              for r in range(k)]

                        @plsc.parallel_loop(0, n_chunks)
                        def _(c):
                            wo = pl.multiple_of(c * SC_LANES, SC_LANES)
                            los, his = [], []
                            for r in range(0, k, 2):
                                lo, hi = _mul2_packed(ring[slot, r, pl.ds(wo, SC_LANES)], ws[r],
                                                      ring[slot, r + 1, pl.ds(wo, SC_LANES)], ws[r + 1])
                                los.append(lo)
                                his.append(hi)
                            while len(los) > 1:
                                los = [los[i] + los[i + 1] for i in range(0, len(los), 2)]
                                his = [his[i] + his[i + 1] for i in range(0, len(his), 2)]
                            plsc.addupdate(out_v.at[tok, pl.ds(wo, SC_LANES)], los[0])
                            plsc.addupdate(out_v.at[tok, pl.ds(dw + wo, SC_LANES)], his[0])

            pltpu.sync_copy(out_v, out_hbm.at[pl.ds(base, tb)])

    return run(u_pk, v_pk, idx, gw, xn_pk, xres)


def _final_norm_kernel(x_ref, g_ref, out_ref):
    out_ref[...] = _rms(x_ref[...], g_ref[...])


def _final_norm(x2, g, tm):
    t, d = x2.shape
    row = pl.BlockSpec((tm, d), lambda i: (i, 0))
    return pl.pallas_call(
        _final_norm_kernel,
        grid=(t // tm,),
        in_specs=[row, pl.BlockSpec((1, d), lambda i: (0, 0))],
        out_specs=row,
        out_shape=jax.ShapeDtypeStruct((t, d), F32),
        compiler_params=_cparams("arbitrary"),
        name="final_norm",
    )(x2, g)


def _row_tile(t, want):
    tm = min(want, t)
    assert t % tm == 0
    return tm


def kernel(x, norm_mix_g, w_in, gm_norm_g, gm_w_s, gm_b_s, ml_conv_w, ml_conv_b, ml_w_q, ml_w_k, ml_w_v, ml_w_gate, ml_b_gate, ml_head_g, ml_skip, w_branch_a, w_branch_b, w_out, norm_ffn_g, peer_w_query, peer_sub_keys, peer_u, peer_v, final_g):
    b, s, d = x.shape
    depth = w_in.shape[0]
    assert s % CHUNK == 0 and d % (ML_HEADS * CHUNK) == 0 and w_in.shape[2] == N_IN_SEG * d
    n_groups = BATCH_GROUPS if b % BATCH_GROUPS == 0 else 1
    bg = b // n_groups
    t = bg * s
    tm_merge = _row_tile(t, 512)
    tm_route = _row_tile(t, 256)
    parts = [x[i * bg:(i + 1) * bg].reshape(t, d) for i in range(n_groups)]
    for l in range(depth):
        w_in_b = w_in[l].astype(BF16)
        w_s_b = gm_w_s[l].astype(BF16)
        wq_b, wk_b, wv_b = (ml_w_q[l].astype(BF16), ml_w_k[l].astype(BF16), ml_w_v[l].astype(BF16))
        wg = ml_w_gate[l].astype(BF16)
        wa_b, wb_b, wo_b = (w_branch_a[l].astype(BF16), w_branch_b[l].astype(BF16),
                            w_out[l].astype(BF16))
        wquery_b = peer_w_query[l].astype(BF16)
        keys = peer_sub_keys[l].reshape(2 * PEER_HEADS, CHUNK, -1).astype(BF16)
        u_pk, v_pk = _pack_rows(peer_u[l]), _pack_rows(peer_v[l])
        for p in range(n_groups):
            x2 = parts[p]
            z_all = _in_proj(
                x2.reshape(bg, s, d), norm_mix_g[l].reshape(1, d), w_in_b,
                gm_norm_g[l].reshape(1, d), w_s_b, gm_b_s[l].T)
            yb = _mlstm(
                z_all, ml_conv_w[l], ml_conv_b[l].reshape(1, d), wq_b, wk_b, wv_b,
                wg, wg.T, ml_b_gate[l].reshape(1, -1), ml_b_gate[l].reshape(-1, 1),
                ml_head_g[l].reshape(1, d), ml_skip[l].reshape(1, d))
            x2 = _merge(x2, z_all, yb.reshape(t, d), wa_b, wb_b, wo_b, tm_merge)
            xn_pk, idx, gw = _peer_route(x2, norm_ffn_g[l].reshape(1, d), wquery_b, keys, tm_route)
            parts[p] = _peer_experts(u_pk, v_pk, idx, gw, xn_pk, x2)
    outs = [_final_norm(p, final_g.reshape(1, d), tm_merge) for p in parts]
    return jnp.concatenate(outs, axis=0).reshape(b, s, d)
```

```python
import functools

import jax
import jax.numpy as jnp
from jax import lax
from jax.experimental import pallas as pl
from jax.experimental.pallas import tpu as pltpu
from jax.experimental.pallas import tpu_sc as plsc

F32 = jnp.float32
BF16 = jnp.bfloat16
I32 = jnp.int32

EPS = 1e-6
NEG = -1e30
CHUNK = 128
GM_GROUPS = 8
ML_HEADS = 4
ML_CONV = 5
CONV_PAD = 8
PEER_HEADS = 8
PEER_TOPK = 16
N_IN_SEG = 6
BATCH_GROUPS = 16
GELU_C = 0.7978845608028654

VMEM_LIMIT = 56 * 1024 * 1024


def _cparams(*sem):
    return pltpu.CompilerParams(dimension_semantics=sem, vmem_limit_bytes=VMEM_LIMIT)


def _rows(c, n=CHUNK):
    return pl.ds(pl.multiple_of(c * n, n), n)


def _rms(x, g):
    return x * lax.rsqrt(jnp.mean(x * x, axis=-1, keepdims=True) + EPS) * g


def _dot(a, b):
    return jnp.dot(a, b, preferred_element_type=F32)


def _dot_nt(a, b):
    return lax.dot_general(a, b, (((1,), (1,)), ((), ())), preferred_element_type=F32)


def _dot_tn(a, b):
    return lax.dot_general(a, b, (((0,), (0,)), ((), ())), preferred_element_type=F32)


def _in_proj_kernel(x_ref, g_ref, w_ref, gmg_ref, ws_ref, bst_ref, out_ref, hn_s, gu_s):
    j = pl.program_id(1)
    n_chunks = x_ref.shape[1] // CHUNK
    d = x_ref.shape[2]

    def z_of(c):
        return _dot(hn_s[_rows(c), :], w_ref[...])

    def for_chunks(fn):
        def body(c, carry):
            fn(c)
            return carry
        lax.fori_loop(0, n_chunks, body, 0)

    @pl.when(j == 0)
    def _():
        def norm(c):
            hn_s[_rows(c), :] = _rms(x_ref[0, _rows(c), :], g_ref[...]).astype(BF16)
        for_chunks(norm)

        def seg_u(c):
            gu_s[_rows(c), :] = jax.nn.gelu(z_of(c)).astype(BF16)
        for_chunks(seg_u)

    @pl.when(j == 1)
    def _():
        def seg_v(c):
            vn = _rms(jax.nn.gelu(z_of(c)), gmg_ref[...]).astype(BF16)
            gu = gu_s[_rows(c), :]
            for g in range(GM_GROUPS):
                cs = slice(g * (d // GM_GROUPS), (g + 1) * (d // GM_GROUPS))
                mixed = _dot(ws_ref[g], vn[:, cs]) + bst_ref[:, g:g + 1]
                out_ref[0, 0, _rows(c), cs] = (gu[:, cs].astype(F32) * mixed).astype(BF16)
        for_chunks(seg_v)

    @pl.when(j == 2)
    def _():
        def seg_xm(c):
            out_ref[0, 0, _rows(c), :] = z_of(c).astype(BF16)
        for_chunks(seg_xm)

    @pl.when(j >= 3)
    def _():
        def seg_gate(c):
            out_ref[0, 0, _rows(c), :] = jax.nn.sigmoid(z_of(c)).astype(BF16)
        for_chunks(seg_gate)


def _in_proj(x3, g, w_in, gm_g, w_s, b_st):
    b, s, d = x3.shape
    return pl.pallas_call(
        _in_proj_kernel,
        grid=(b, N_IN_SEG),
        in_specs=[
            pl.BlockSpec((1, s, d), lambda i, j: (i, 0, 0)),
            pl.BlockSpec((1, d), lambda i, j: (0, 0)),
            pl.BlockSpec((d, d), lambda i, j: (0, j)),
            pl.BlockSpec((1, d), lambda i, j: (0, 0)),
            pl.BlockSpec((GM_GROUPS, CHUNK, CHUNK), lambda i, j: (0, 0, 0)),
            pl.BlockSpec((CHUNK, GM_GROUPS), lambda i, j: (0, 0)),
        ],
        out_specs=pl.BlockSpec((1, 1, s, d), lambda i, j: (jnp.maximum(j - 1, 0), i, 0, 0)),
        out_shape=jax.ShapeDtypeStruct((N_IN_SEG - 1, b, s, d), BF16),
        scratch_shapes=[pltpu.VMEM((s, d), BF16), pltpu.VMEM((s, d), BF16)],
        compiler_params=_cparams("arbitrary", "arbitrary"),
        name="in_proj",
    )(x3, g, w_in, gm_g, w_s, b_st)


def _split3(a):
    hi = a.astype(BF16)
    r1 = a - hi.astype(F32)
    mid = r1.astype(BF16)
    lo = (r1 - mid.astype(F32)).astype(BF16)
    return hi, mid, lo


def _log_sigmoid(x):
    return jnp.minimum(x, 0.0) - jnp.log(1.0 + jnp.exp(-jnp.abs(x)))


def _mlstm_kernel(xm_ref, so_ref, cw_ref, cb_ref, wq_ref, wk_ref, wv_ref, wg_ref, wgt_ref,
                  bg_ref, bgt_ref, hg_ref, sk_ref, out_ref,
                  pad_s, xc_s, gc_s, gr_s, q_s, k_s, v_s, hf_s, ct_s):
    s = xm_ref.shape[2]
    d = xm_ref.shape[3]
    dh = d // ML_HEADS
    n_chunks = s // CHUNK
    nh = ML_HEADS

    ri = lax.broadcasted_iota(I32, (CHUNK, CHUNK), 0)
    ci = lax.broadcasted_iota(I32, (CHUNK, CHUNK), 1)
    tril = ri >= ci
    triu = ri <= ci
    tril_b = jnp.where(tril, 1.0, 0.0).astype(BF16)
    triu_b = jnp.where(triu, 1.0, 0.0).astype(BF16)

    pad_s[0:CONV_PAD, :] = jnp.zeros((CONV_PAD, d), F32)
    pad_s[s + CONV_PAD:s + 2 * CONV_PAD, :] = jnp.zeros((CONV_PAD, d), F32)

    def fill(c, carry):
        pad_s[pl.ds(pl.multiple_of(c * CHUNK, CHUNK) + CONV_PAD, CHUNK), :] = (
            xm_ref[0, 0, _rows(c), :].astype(F32))
        return carry
    lax.fori_loop(0, n_chunks, fill, 0)

    lane = lax.broadcasted_iota(I32, (s, CHUNK), 1)
    v_s[:, dh:dh + CHUNK] = jnp.where(lane == 0, 1.0, 0.0).astype(BF16)

    def conv_gates(c, carry):
        base = pl.multiple_of(c * CHUNK, CHUNK)
        gc = jnp.zeros((CHUNK, 4 * nh), F32) + bg_ref[...]
        gr = jnp.zeros((4 * nh, CHUNK), F32) + bgt_ref[...]
        for hh in range(nh):
            cs = slice(hh * dh, (hh + 1) * dh)
            acc = jnp.zeros((CHUNK, dh), F32) + cb_ref[:, cs]
            win = pad_s[pl.ds(base, CHUNK + 2 * CONV_PAD), cs]
            for t in range(ML_CONV):
                off = CONV_PAD - ML_CONV // 2 + t
                acc = acc + win[off:off + CHUNK, :] * cw_ref[t:t + 1, cs]
            xc = (acc * jax.nn.sigmoid(acc)).astype(BF16)
            xc_s[_rows(c), cs] = xc
            gc = gc + _dot(xc, wg_ref[cs, :])
            gr = gr + _dot_nt(wgt_ref[:, cs], xc)
        lf_c = _log_sigmoid(gc[:, nh:2 * nh])
        lb_c = _log_sigmoid(gc[:, 3 * nh:4 * nh])
        lf_r = _log_sigmoid(gr[nh:2 * nh, :])
        lb_r = _log_sigmoid(gr[3 * nh:4 * nh, :])
        bcf = sum(_dot(tril_b, p) for p in _split3(lf_c))
        bcb = sum(_dot(triu_b, p) for p in _split3(lb_c))
        brf = sum(_dot(p, triu_b) for p in _split3(lf_r))
        brb = sum(_dot(p, tril_b) for p in _split3(lb_r))
        gc_s[_rows(c), 0:nh] = gc[:, 0:nh]
        gc_s[_rows(c), nh:2 * nh] = bcf
        gc_s[_rows(c), 2 * nh:3 * nh] = gc[:, 2 * nh:3 * nh]
        gc_s[_rows(c), 3 * nh:4 * nh] = bcb
        gr_s[0:nh, _rows(c)] = gr[0:nh, :]
        gr_s[nh:2 * nh, _rows(c)] = brf
        gr_s[2 * nh:3 * nh, _rows(c)] = gr[2 * nh:3 * nh, :]
        gr_s[3 * nh:4 * nh, _rows(c)] = brb
        return carry
    lax.fori_loop(0, n_chunks, conv_gates, 0)

    for hh in range(nh):
        cs = slice(hh * dh, (hh + 1) * dh)

        def qkv(c, carry):
            xc = xc_s[_rows(c), cs]
            q_s[_rows(c), :] = _dot(xc, wq_ref[hh]).astype(BF16)
            k_s[_rows(c), :] = (_dot(xc, wk_ref[hh]) * (dh ** -0.5)).astype(BF16)
            v_s[_rows(c), 0:dh] = _dot(xm_ref[0, 0, _rows(c), cs], wv_ref[hh]).astype(BF16)
            return carry
        lax.fori_loop(0, n_chunks, qkv, 0)

        def step(c, m, i_col, b_col, mask, last_row):
            r = _rows(c)
            qc, kc, vx = q_s[r, :], k_s[r, :], v_s[r, :]
            bc = gc_s[r, b_col:b_col + 1]
            ic = gc_s[r, i_col:i_col + 1]
            br = gr_s[b_col:b_col + 1, r]
            ir = gr_s[i_col:i_col + 1, r]
            dm = jnp.where(mask, bc - br + ir, NEG)
            inter = bc + m
            m_t = jnp.maximum(inter, jnp.max(dm, axis=-1, keepdims=True))
            w_intra = jnp.exp(dm - m_t)
            w_inter = jnp.exp(inter - m_t)
            sc = (_dot_nt(qc, kc) * w_intra).astype(BF16)
            res = _dot(sc, vx) + w_inter * _dot(qc, ct_s[...].astype(BF16))
            num = res[:, 0:dh]
            den = res[:, dh:dh + 1]
            h = num / jnp.maximum(jnp.abs(den), jnp.exp(-m_t))
            b_l = bc[last_row:last_row + 1, :]
            g_col = b_l - bc + ic
            m_new = jnp.maximum(b_l + m, jnp.max(g_col, axis=0, keepdims=True))
            w_s = jnp.exp(g_col - m_new)
            w_c = jnp.exp(b_l + m - m_new)
            ct_s[...] = w_c * ct_s[...] + _dot_tn(kc, (w_s * vx.astype(F32)).astype(BF16))
            return h, m_new

        ct_s[...] = jnp.zeros(ct_s.shape, F32)

        def fwd(c, m):
            h, m_new = step(c, m, hh, nh + hh, tril, CHUNK - 1)
            hf_s[_rows(c), :] = h
            return m_new
        lax.fori_loop(0, n_chunks, fwd, jnp.zeros((1, 1), F32))

        ct_s[...] = jnp.zeros(ct_s.shape, F32)

        def bwd(i, m):
            c = n_chunks - 1 - i
            h, m_new = step(c, m, 2 * nh + hh, 3 * nh + hh, triu, 0)
            h = h + hf_s[_rows(c), :]
            y = _rms(h, hg_ref[:, cs])
            xc = xc_s[_rows(c), cs].astype(F32)
            so = so_ref[0, 0, _rows(c), cs].astype(F32)
            out_ref[0, _rows(c), cs] = (so * (y + sk_ref[:, cs] * xc)).astype(BF16)
            return m_new
        lax.fori_loop(0, n_chunks, bwd, jnp.zeros((1, 1), F32))


def _mlstm(z_all, cw, cb, wq, wk, wv, wg, wgt, bg, bgt, hg, sk):
    _, b, s, d = z_all.shape
    dh = d // ML_HEADS
    ng = 4 * ML_HEADS

    def full(a):
        return pl.BlockSpec(a.shape, lambda i, _n=a.ndim: (0,) * _n)

    return pl.pallas_call(
        _mlstm_kernel,
        grid=(b,),
        in_specs=[
            pl.BlockSpec((1, 1, s, d), lambda i: (1, i, 0, 0)),
            pl.BlockSpec((1, 1, s, d), lambda i: (2, i, 0, 0)),
            full(cw), full(cb), full(wq), full(wk), full(wv), full(wg), full(wgt),
            full(bg), full(bgt), full(hg), full(sk),
        ],
        out_specs=pl.BlockSpec((1, s, d), lambda i: (i, 0, 0)),
        out_shape=jax.ShapeDtypeStruct((b, s, d), BF16),
        scratch_shapes=[
            pltpu.VMEM((s + 2 * CONV_PAD, d), F32),
            pltpu.VMEM((s, d), BF16),
            pltpu.VMEM((s, ng), F32),
            pltpu.VMEM((ng, s), F32),
            pltpu.VMEM((s, dh), BF16),
            pltpu.VMEM((s, dh), BF16),
            pltpu.VMEM((s, dh + CHUNK), BF16),
            pltpu.VMEM((s, dh), F32),
            pltpu.VMEM((dh, dh + CHUNK), F32),
        ],
        compiler_params=_cparams("arbitrary"),
        name="mlstm",
    )(z_all, z_all, cw, cb, wq, wk, wv, wg, wgt, bg, bgt, hg, sk)


def _merge_kernel(x_ref, ya_ref, yb_ref, ga_ref, gb_ref, wa_ref, wb_ref, wo_ref, out_ref):
    merged = (ga_ref[0].astype(F32) * _dot(ya_ref[0], wa_ref[...])
              + gb_ref[0].astype(F32) * _dot(yb_ref[...], wb_ref[...]))
    out_ref[...] = x_ref[...] + _dot(merged.astype(BF16), wo_ref[...])


def _merge(x2, z_all, yb2, wa, wb, wo, tm):
    t, d = x2.shape
    z3 = z_all.reshape(z_all.shape[0], t, d)

    def slab(k):
        return pl.BlockSpec((1, tm, d), lambda i, _k=k: (_k, i, 0))

    row = pl.BlockSpec((tm, d), lambda i: (i, 0))
    wspec = pl.BlockSpec((d, d), lambda i: (0, 0))
    return pl.pallas_call(
        _merge_kernel,
        grid=(t // tm,),
        in_specs=[row, slab(0), row, slab(3), slab(4), wspec, wspec, wspec],
        out_specs=row,
        out_shape=jax.ShapeDtypeStruct((t, d), F32),
        compiler_params=_cparams("arbitrary"),
        name="merge",
    )(x2, z3, yb2, z3, z3, wa, wb, wo)


def _peer_route_kernel(x_ref, g_ref, wq_ref, keys_ref, xpk_ref, idx_ref, gw_ref,
                       q_s, sv_s, si_s, cv_s, ce_s, it_s, gt_s):
    tm = x_ref.shape[0]
    dw = x_ref.shape[1] // 2
    k = PEER_TOPK
    big = 1e9

    xb = _rms(x_ref[...], g_ref[...]).astype(BF16)
    bits = lax.bitcast_convert_type(xb.astype(F32), I32)
    xpk_ref[...] = lax.shift_right_logical(bits[:, :dw], jnp.full((tm, dw), 16, I32)) | bits[:, dw:]
    q_s[...] = _dot(xb, wq_ref[...]).astype(BF16)

    def sub_topk(hp, carry):
        qh = q_s[:, pl.ds(pl.multiple_of(hp * CHUNK, CHUNK), CHUNK)]
        sc = _dot_nt(keys_ref[hp], qh)
        rowf = lax.broadcasted_iota(I32, sc.shape, 0).astype(F32)
        for j in range(k):
            m = jnp.max(sc, axis=0, keepdims=True)
            i = jnp.min(jnp.where(sc == m, rowf, big), axis=0, keepdims=True)
            sv_s[hp, j:j + 1, :] = m
            si_s[hp, j:j + 1, :] = i
            sc = jnp.where(rowf == i, -jnp.inf, sc)
        return carry
    lax.fori_loop(0, 2 * PEER_HEADS, sub_topk, 0)

    def head_topk(h, carry):
        sv0, sv1 = sv_s[2 * h], sv_s[2 * h + 1]
        si0, si1 = si_s[2 * h], si_s[2 * h + 1]
        half = k // 2
        cand = jnp.concatenate(
            [sv0[0:1, :] + sv1] + [sv0[a:a + 1, :] + sv1[0:half, :] for a in range(1, half)]
            + [sv0[half:k, :] + sv1[0:1, :]], axis=0)
        eid = jnp.concatenate(
            [si0[0:1, :] * float(CHUNK) + si1]
            + [si0[a:a + 1, :] * float(CHUNK) + si1[0:half, :] for a in range(1, half)]
            + [si0[half:k, :] * float(CHUNK) + si1[0:1, :]], axis=0)
        row = lax.broadcasted_iota(I32, cand.shape, 0)
        n_mid = k + half * (half - 1)
        flat = jnp.where(row < k, row,
                         jnp.where(row < n_mid, row + half * ((row - k) // half),
                                   k * (row - n_mid + half)))
        rowf = flat.astype(F32)
        for j in range(k):
            m = jnp.max(cand, axis=0, keepdims=True)
            i = jnp.min(jnp.where(cand == m, rowf, big), axis=0, keepdims=True)
            hit = rowf == i
            cv_s[j:j + 1, :] = m
            ce_s[j:j + 1, :] = jnp.sum(jnp.where(hit, eid, 0.0), axis=0, keepdims=True)
            cand = jnp.where(hit, -jnp.inf, cand)
        cv = cv_s[...]
        e = jnp.exp(cv - cv[0:1, :])
        rows = pl.ds(pl.multiple_of(h * k, k), k)
        gt_s[rows, :] = e / jnp.sum(e, axis=0, keepdims=True)
        it_s[rows, :] = ce_s[...]
        return carry
    lax.fori_loop(0, PEER_HEADS, head_topk, 0)

    idx_ref[...] = it_s[...].T.astype(I32)
    gw_ref[...] = gt_s[...].T


def _peer_route(x2, g, wq, keys, tm):
    t, d = x2.shape
    qd = wq.shape[1]
    nsel = PEER_HEADS * PEER_TOPK
    row = pl.BlockSpec((tm, d), lambda i: (i, 0))
    sel = pl.BlockSpec((tm, nsel), lambda i: (i, 0))
    return pl.pallas_call(
        _peer_route_kernel,
        grid=(t // tm,),
        in_specs=[
            row,
            pl.BlockSpec((1, d), lambda i: (0, 0)),
            pl.BlockSpec((d, qd), lambda i: (0, 0)),
            pl.BlockSpec(keys.shape, lambda i: (0, 0, 0)),
        ],
        out_specs=[pl.BlockSpec((tm, d // 2), lambda i: (i, 0)), sel, sel],
        out_shape=[
            jax.ShapeDtypeStruct((t, d // 2), I32),
            jax.ShapeDtypeStruct((t, nsel), I32),
            jax.ShapeDtypeStruct((t, nsel), F32),
        ],
        scratch_shapes=[
            pltpu.VMEM((tm, qd), BF16),
            pltpu.VMEM((2 * PEER_HEADS, PEER_TOPK, tm), F32),
            pltpu.VMEM((2 * PEER_HEADS, PEER_TOPK, tm), F32),
            pltpu.VMEM((PEER_TOPK, tm), F32),
            pltpu.VMEM((PEER_TOPK, tm), F32),
            pltpu.VMEM((nsel, tm), F32),
            pltpu.VMEM((nsel, tm), F32),
        ],
        compiler_params=_cparams("arbitrary"),
        name="peer_route",
    )(x2, g, wq, keys)


SC_LANES = 16
SC_TOK_BLOCK = 32
SC_RING = 8
SC_AHEAD = 7


def _pack_rows(tab):
    d = tab.shape[1]
    bits = lax.bitcast_convert_type(tab.astype(BF16), jnp.uint16).astype(jnp.uint32)
    return lax.bitcast_convert_type(bits[:, :d // 2] | (bits[:, d // 2:] << 16), I32)


def _unpack_words(w):
    lo = lax.bitcast_convert_type(lax.shift_left(w, jnp.full(w.shape, 16, I32)), F32)
    hi = lax.bitcast_convert_type(w & jnp.full(w.shape, -65536, I32), F32)
    return lo, hi


def _mul2_packed(a0, b0, a1, b1):
    p = (plsc.bitcast(a0, BF16) * plsc.bitcast(b0, BF16)
         + plsc.bitcast(a1, BF16) * plsc.bitcast(b1, BF16))
    return _unpack_words(plsc.bitcast(p, I32))


def _round_to_packed_pair(x):
    bits = lax.bitcast_convert_type(x, I32)
    odd = lax.shift_right_logical(bits, jnp.full(x.shape, 16, I32)) & jnp.full(x.shape, 1, I32)
    hi = (bits + jnp.full(x.shape, 0x7FFF, I32) + odd) & jnp.full(x.shape, -65536, I32)
    return hi | lax.shift_right_logical(hi, jnp.full(x.shape, 16, I32))


def _peer_experts(u_pk, v_pk, idx, gw, xn_pk, xres):
    t, d = xres.shape
    k = PEER_TOPK
    nh = PEER_HEADS
    dw = d // 2
    n_steps = 2 * nh
    assert k == SC_LANES and n_steps % SC_RING == 0 and SC_AHEAD < SC_RING and dw % (2 * SC_LANES) == 0
    info = plsc.get_sparse_core_info()
    nc, ns = info.num_cores, info.num_subcores
    nw = nc * ns
    tb = SC_TOK_BLOCK
    assert t % (nw * tb) == 0
    tpw = t // nw
    n_chunks = dw // SC_LANES
    mesh = plsc.VectorSubcoreMesh(core_axis_name="c", subcore_axis_name="s")

    @functools.partial(
        pl.kernel,
        mesh=mesh,
        out_type=jax.ShapeDtypeStruct((t, d), F32),
        scratch_types=[
            pltpu.VMEM((tb, nh * k), I32),
            pltpu.VMEM((tb, nh * k), F32),
            pltpu.VMEM((tb, dw), I32),
            pltpu.VMEM((tb, d), F32),
            pltpu.VMEM((SC_RING, k, dw), I32),
            pltpu.VMEM((k, SC_LANES), F32),
            pltpu.VMEM((nh, k), F32),
            pltpu.VMEM((nh, k), I32),
            pltpu.SemaphoreType.DMA((SC_RING,)),
            pltpu.SemaphoreType.DMA((4,)),
        ],
        compiler_params=pltpu.CompilerParams(needs_layout_passes=False),
        name="peer_experts",
    )
    def run(u_hbm, v_hbm, idx_hbm, gw_hbm, xn_hbm, xres_hbm, out_hbm,
            idx_v, gw_v, xn_v, out_v, ring, red_v, act_v, w_v, sems, in_sems):
        wid = lax.axis_index("s") * nc + lax.axis_index("c")
        tok0 = wid * tpw
        lane = lax.iota(I32, SC_LANES)

        def gather(tok, step):
            tab = u_hbm if step < nh else v_hbm
            slot = step % SC_RING
            return pltpu.make_async_copy(
                tab.at[idx_v.at[tok, pl.ds((step % nh) * k, k)]], ring.at[slot], sems.at[slot])

        @pl.loop(0, tpw // tb)
        def _(blk):
            base = tok0 + blk * tb
            loads = [
                pltpu.make_async_copy(src.at[pl.ds(base, tb)], dst, in_sems.at[i])
                for i, (src, dst) in enumerate(
                    ((idx_hbm, idx_v), (gw_hbm, gw_v), (xn_hbm, xn_v), (xres_hbm, out_v)))]
            for cp in loads:
                cp.start()
            loads[0].wait()
            for step in range(SC_AHEAD):
                gather(0, step).start()
            for cp in loads[1:]:
                cp.wait()

            @pl.loop(0, tb)
            def _(tok):
                for step in range(n_steps):
                    h = step % nh
                    slot = step % SC_RING
                    ahead = step + SC_AHEAD
                    if ahead < n_steps:
                        gather(tok, ahead).start()
                    else:
                        @pl.when(tok + 1 < tb)
                        def _():
                            gather(tok + 1, ahead - n_steps).start()

                    if step == nh:
                        for hh in range(nh):
                            act = act_v[hh, :]
                            z = GELU_C * (act + 0.044715 * act * act * act)
                            tanh_z = 1.0 - 2.0 / (jnp.exp(2.0 * z) + 1.0)
                            w_v[hh, :] = _round_to_packed_pair(
                                gw_v[tok, pl.ds(hh * k, k)] * (0.5 * act * (1.0 + tanh_z)))

                    gather(tok, step).wait()

                    if step < nh:
                        def dot_rows(c, accs):
                            w0 = pl.multiple_of(c * 2 * SC_LANES, 2 * SC_LANES)
                            w1 = w0 + SC_LANES
                            x0 = xn_v[tok, pl.ds(w0, SC_LANES)]
                            x1 = xn_v[tok, pl.ds(w1, SC_LANES)]
                            out = []
                            for r in range(k):
                                lo, hi = _mul2_packed(ring[slot, r, pl.ds(w0, SC_LANES)], x0,
                                                      ring[slot, r, pl.ds(w1, SC_LANES)], x1)
                                out.append(accs[r] + (lo + hi))
                            return tuple(out)
                        accs = lax.fori_loop(
                            0, n_chunks // 2, dot_rows,
                            tuple(jnp.zeros((SC_LANES,), F32) for _ in range(k)))
                        for r in range(k):
                            red_v[r, :] = accs[r]
                        cols = [plsc.load_gather(red_v, [lane, jnp.full((SC_LANES,), jj, I32)])
                                for jj in range(SC_LANES)]
                        while len(cols) > 1:
                            cols = [cols[i] + cols[i + 1] for i in range(0, len(cols), 2)]
                        act_v[h, :] = cols[0]
                    else:
                        wv = w_v[h, :]
                        ws = [wv.at[jnp.full((SC_LANES,), r, I32)].get(mode="promise_in_bounds")
                              for r in range(k)]

                        @plsc.parallel_loop(0, n_chunks)
                        def _(c):
                            wo = pl.multiple_of(c * SC_LANES, SC_LANES)
                            los, his = [], []
                            for r in range(0, k, 2):
                                lo, hi = _mul2_packed(ring[slot, r, pl.ds(wo, SC_LANES)], ws[r],
                                                      ring[slot, r + 1, pl.ds(wo, SC_LANES)], ws[r + 1])
                                los.append(lo)
                                his.append(hi)
                            while len(los) > 1:
                                los = [los[i] + los[i + 1] for i in range(0, len(los), 2)]
                                his = [his[i] + his[i + 1] for i in range(0, len(his), 2)]
                            plsc.addupdate(out_v.at[tok, pl.ds(wo, SC_LANES)], los[0])
                            plsc.addupdate(out_v.at[tok, pl.ds(dw + wo, SC_LANES)], his[0])

            pltpu.sync_copy(out_v, out_hbm.at[pl.ds(base, tb)])

    return run(u_pk, v_pk, idx, gw, xn_pk, xres)


def _final_norm_kernel(x_ref, g_ref, out_ref):
    out_ref[...] = _rms(x_ref[...], g_ref[...])


def _final_norm(x2, g, tm):
    t, d = x2.shape
    row = pl.BlockSpec((tm, d), lambda i: (i, 0))
    return pl.pallas_call(
        _final_norm_kernel,
        grid=(t // tm,),
        in_specs=[row, pl.BlockSpec((1, d), lambda i: (0, 0))],
        out_specs=row,
        out_shape=jax.ShapeDtypeStruct((t, d), F32),
        compiler_params=_cparams("arbitrary"),
        name="final_norm",
    )(x2, g)


def _row_tile(t, want):
    tm = min(want, t)
    assert t % tm == 0
    return tm


def kernel(x, norm_mix_g, w_in, gm_norm_g, gm_w_s, gm_b_s, ml_conv_w, ml_conv_b, ml_w_q, ml_w_k, ml_w_v, ml_w_gate, ml_b_gate, ml_head_g, ml_skip, w_branch_a, w_branch_b, w_out, norm_ffn_g, peer_w_query, peer_sub_keys, peer_u, peer_v, final_g):
    b, s, d = x.shape
    depth = w_in.shape[0]
    assert s % CHUNK == 0 and d % (ML_HEADS * CHUNK) == 0 and w_in.shape[2] == N_IN_SEG * d
    n_groups = BATCH_GROUPS if b % BATCH_GROUPS == 0 else 1
    bg = b // n_groups
    t = bg * s
    tm_merge = _row_tile(t, 512)
    tm_route = _row_tile(t, 256)
    parts = [x[i * bg:(i + 1) * bg].reshape(t, d) for i in range(n_groups)]
    for l in range(depth):
        w_in_b = w_in[l].astype(BF16)
        w_s_b = gm_w_s[l].astype(BF16)
        wq_b, wk_b, wv_b = (ml_w_q[l].astype(BF16), ml_w_k[l].astype(BF16), ml_w_v[l].astype(BF16))
        wg = ml_w_gate[l].astype(BF16)
        wa_b, wb_b, wo_b = (w_branch_a[l].astype(BF16), w_branch_b[l].astype(BF16),
                            w_out[l].astype(BF16))
        wquery_b = peer_w_query[l].astype(BF16)
        keys = peer_sub_keys[l].reshape(2 * PEER_HEADS, CHUNK, -1).astype(BF16)
        u_pk, v_pk = _pack_rows(peer_u[l]), _pack_rows(peer_v[l])
        for p in range(n_groups):
            x2 = parts[p]
            z_all = _in_proj(
                x2.reshape(bg, s, d), norm_mix_g[l].reshape(1, d), w_in_b,
                gm_norm_g[l].reshape(1, d), w_s_b, gm_b_s[l].T)
            yb = _mlstm(
                z_all, ml_conv_w[l], ml_conv_b[l].reshape(1, d), wq_b, wk_b, wv_b,
                wg, wg.T, ml_b_gate[l].reshape(1, -1), ml_b_gate[l].reshape(-1, 1),
                ml_head_g[l].reshape(1, d), ml_skip[l].reshape(1, d))
            x2 = _merge(x2, z_all, yb.reshape(t, d), wa_b, wb_b, wo_b, tm_merge)
            xn_pk, idx, gw = _peer_route(x2, norm_ffn_g[l].reshape(1, d), wquery_b, keys, tm_route)
            parts[p] = _peer_experts(u_pk, v_pk, idx, gw, xn_pk, x2)
    outs = [_final_norm(p, final_g.reshape(1, d), tm_merge) for p in parts]
    return jnp.concatenate(outs, axis=0).reshape(b, s, d)
```

```python
import functools

import jax
import jax.numpy as jnp
from jax import lax
from jax.experimental import pallas as pl
from jax.experimental.pallas import tpu as pltpu
from jax.experimental.pallas import tpu_sc as plsc

F32 = jnp.float32
BF16 = jnp.bfloat16
I32 = jnp.int32

EPS = 1e-6
NEG = -1e30
CHUNK = 128
GM_GROUPS = 8
ML_HEADS = 4
ML_CONV = 5
CONV_PAD = 8
PEER_HEADS = 8
PEER_TOPK = 16
N_IN_SEG = 6
BATCH_GROUPS = 16
GELU_C = 0.7978845608028654

VMEM_LIMIT = 56 * 1024 * 1024


def _cparams(*sem):
    return pltpu.CompilerParams(dimension_semantics=sem, vmem_limit_bytes=VMEM_LIMIT)


def _rows(c, n=CHUNK):
    return pl.ds(pl.multiple_of(c * n, n), n)


def _rms(x, g):
    return x * lax.rsqrt(jnp.mean(x * x, axis=-1, keepdims=True) + EPS) * g


def _dot(a, b):
    return jnp.dot(a, b, preferred_element_type=F32)


def _dot_nt(a, b):
    return lax.dot_general(a, b, (((1,), (1,)), ((), ())), preferred_element_type=F32)


def _dot_tn(a, b):
    return lax.dot_general(a, b, (((0,), (0,)), ((), ())), preferred_element_type=F32)


def _in_proj_kernel(x_ref, g_ref, w_ref, gmg_ref, ws_ref, bst_ref, out_ref, hn_s, gu_s):
    j = pl.program_id(1)
    n_chunks = x_ref.shape[1] // CHUNK
    d = x_ref.shape[2]

    def z_of(c):
        return _dot(hn_s[_rows(c), :], w_ref[...])

    def for_chunks(fn):
        def body(c, carry):
            fn(c)
            return carry
        lax.fori_loop(0, n_chunks, body, 0)

    @pl.when(j == 0)
    def _():
        def norm(c):
            hn_s[_rows(c), :] = _rms(x_ref[0, _rows(c), :], g_ref[...]).astype(BF16)
        for_chunks(norm)

        def seg_u(c):
            gu_s[_rows(c), :] = jax.nn.gelu(z_of(c)).astype(BF16)
        for_chunks(seg_u)

    @pl.when(j == 1)
    def _():
        def seg_v(c):
            vn = _rms(jax.nn.gelu(z_of(c)), gmg_ref[...]).astype(BF16)
            gu = gu_s[_rows(c), :]
            for g in range(GM_GROUPS):
                cs = slice(g * (d // GM_GROUPS), (g + 1) * (d // GM_GROUPS))
                mixed = _dot(ws_ref[g], vn[:, cs]) + bst_ref[:, g:g + 1]
                out_ref[0, 0, _rows(c), cs] = (gu[:, cs].astype(F32) * mixed).astype(BF16)
        for_chunks(seg_v)

    @pl.when(j == 2)
    def _():
        def seg_xm(c):
            out_ref[0, 0, _rows(c), :] = z_of(c).astype(BF16)
        for_chunks(seg_xm)

    @pl.when(j >= 3)
    def _():
        def seg_gate(c):
            out_ref[0, 0, _rows(c), :] = jax.nn.sigmoid(z_of(c)).astype(BF16)
        for_chunks(seg_gate)


def _in_proj(x3, g, w_in, gm_g, w_s, b_st):
    b, s, d = x3.shape
    return pl.pallas_call(
        _in_proj_kernel,
        grid=(b, N_IN_SEG),
        in_specs=[
            pl.BlockSpec((1, s, d), lambda i, j: (i, 0, 0)),
            pl.BlockSpec((1, d), lambda i, j: (0, 0)),
            pl.BlockSpec((d, d), lambda i, j: (0, j)),
            pl.BlockSpec((1, d), lambda i, j: (0, 0)),
            pl.BlockSpec((GM_GROUPS, CHUNK, CHUNK), lambda i, j: (0, 0, 0)),
            pl.BlockSpec((CHUNK, GM_GROUPS), lambda i, j: (0, 0)),
        ],
        out_specs=pl.BlockSpec((1, 1, s, d), lambda i, j: (jnp.maximum(j - 1, 0), i, 0, 0)),
        out_shape=jax.ShapeDtypeStruct((N_IN_SEG - 1, b, s, d), BF16),
        scratch_shapes=[pltpu.VMEM((s, d), BF16), pltpu.VMEM((s, d), BF16)],
        compiler_params=_cparams("arbitrary", "arbitrary"),
        name="in_proj",
    )(x3, g, w_in, gm_g, w_s, b_st)


def _split3(a):
    hi = a.astype(BF16)
    r1 = a - hi.astype(F32)
    mid = r1.astype(BF16)
    lo = (r1 - mid.astype(F32)).astype(BF16)
    return hi, mid, lo


def _log_sigmoid(x):
    return jnp.minimum(x, 0.0) - jnp.log(1.0 + jnp.exp(-jnp.abs(x)))


def _mlstm_kernel(xm_ref, so_ref, cw_ref, cb_ref, wq_ref, wk_ref, wv_ref, wg_ref, wgt_ref,
                  bg_ref, bgt_ref, hg_ref, sk_ref, out_ref,
                  pad_s, xc_s, gc_s, gr_s, q_s, k_s, v_s, hf_s, ct_s):
    s = xm_ref.shape[2]
    d = xm_ref.shape[3]
    dh = d // ML_HEADS
    n_chunks = s // CHUNK
    nh = ML_HEADS

    ri = lax.broadcasted_iota(I32, (CHUNK, CHUNK), 0)
    ci = lax.broadcasted_iota(I32, (CHUNK, CHUNK), 1)
    tril = ri >= ci
    triu = ri <= ci
    tril_b = jnp.where(tril, 1.0, 0.0).astype(BF16)
    triu_b = jnp.where(triu, 1.0, 0.0).astype(BF16)

    pad_s[0:CONV_PAD, :] = jnp.zeros((CONV_PAD, d), F32)
    pad_s[s + CONV_PAD:s + 2 * CONV_PAD, :] = jnp.zeros((CONV_PAD, d), F32)

    def fill(c, carry):
        pad_s[pl.ds(pl.multiple_of(c * CHUNK, CHUNK) + CONV_PAD, CHUNK), :] = (
            xm_ref[0, 0, _rows(c), :].astype(F32))
        return carry
    lax.fori_loop(0, n_chunks, fill, 0)

    lane = lax.broadcasted_iota(I32, (s, CHUNK), 1)
    v_s[:, dh:dh + CHUNK] = jnp.where(lane == 0, 1.0, 0.0).astype(BF16)

    def conv_gates(c, carry):
        base = pl.multiple_of(c * CHUNK, CHUNK)
        gc = jnp.zeros((CHUNK, 4 * nh), F32) + bg_ref[...]
        gr = jnp.zeros((4 * nh, CHUNK), F32) + bgt_ref[...]
        for hh in range(nh):
            cs = slice(hh * dh, (hh + 1) * dh)
            acc = jnp.zeros((CHUNK, dh), F32) + cb_ref[:, cs]
            win = pad_s[pl.ds(base, CHUNK + 2 * CONV_PAD), cs]
            for t in range(ML_CONV):
                off = CONV_PAD - ML_CONV // 2 + t
                acc = acc + win[off:off + CHUNK, :] * cw_ref[t:t + 1, cs]
            xc = (acc * jax.nn.sigmoid(acc)).astype(BF16)
            xc_s[_rows(c), cs] = xc
            gc = gc + _dot(xc, wg_ref[cs, :])
            gr = gr + _dot_nt(wgt_ref[:, cs], xc)
        lf_c = _log_sigmoid(gc[:, nh:2 * nh])
        lb_c = _log_sigmoid(gc[:, 3 * nh:4 * nh])
        lf_r = _log_sigmoid(gr[nh:2 * nh, :])
        lb_r = _log_sigmoid(gr[3 * nh:4 * nh, :])
        bcf = sum(_dot(tril_b, p) for p in _split3(lf_c))
        bcb = sum(_dot(triu_b, p) for p in _split3(lb_c))
        brf = sum(_dot(p, triu_b) for p in _split3(lf_r))
        brb = sum(_dot(p, tril_b) for p in _split3(lb_r))
        gc_s[_rows(c), 0:nh] = gc[:, 0:nh]
        gc_s[_rows(c), nh:2 * nh] = bcf
        gc_s[_rows(c), 2 * nh:3 * nh] = gc[:, 2 * nh:3 * nh]
        gc_s[_rows(c), 3 * nh:4 * nh] = bcb
        gr_s[0:nh, _rows(c)] = gr[0:nh, :]
        gr_s[nh:2 * nh, _rows(c)] = brf
        gr_s[2 * nh:3 * nh, _rows(c)] = gr[2 * nh:3 * nh, :]
        gr_s[3 * nh:4 * nh, _rows(c)] = brb
        return carry
    lax.fori_loop(0, n_chunks, conv_gates, 0)

    for hh in range(nh):
        cs = slice(hh * dh, (hh + 1) * dh)

        def qkv(c, carry):
            xc = xc_s[_rows(c), cs]
            q_s[_rows(c), :] = _dot(xc, wq_ref[hh]).astype(BF16)
            k_s[_rows(c), :] = (_dot(xc, wk_ref[hh]) * (dh ** -0.5)).astype(BF16)
            v_s[_rows(c), 0:dh] = _dot(xm_ref[0, 0, _rows(c), cs], wv_ref[hh]).astype(BF16)
            return carry
        lax.fori_loop(0, n_chunks, qkv, 0)

        def step(c, m, i_col, b_col, mask, last_row):
            r = _rows(c)
            qc, kc, vx = q_s[r, :], k_s[r, :], v_s[r, :]
            bc = gc_s[r, b_col:b_col + 1]
            ic = gc_s[r, i_col:i_col + 1]
            br = gr_s[b_col:b_col + 1, r]
            ir = gr_s[i_col:i_col + 1, r]
            dm = jnp.where(mask, bc - br + ir, NEG)
            inter = bc + m
            m_t = jnp.maximum(inter, jnp.max(dm, axis=-1, keepdims=True))
            w_intra = jnp.exp(dm - m_t)
            w_inter = jnp.exp(inter - m_t)
            sc = (_dot_nt(qc, kc) * w_intra).astype(BF16)
            res = _dot(sc, vx) + w_inter * _dot(qc, ct_s[...].astype(BF16))
            num = res[:, 0:dh]
            den = res[:, dh:dh + 1]
            h = num / jnp.maximum(jnp.abs(den), jnp.exp(-m_t))
            b_l = bc[last_row:last_row + 1, :]
            g_col = b_l - bc + ic
            m_new = jnp.maximum(b_l + m, jnp.max(g_col, axis=0, keepdims=True))
            w_s = jnp.exp(g_col - m_new)
            w_c = jnp.exp(b_l + m - m_new)
            ct_s[...] = w_c * ct_s[...] + _dot_tn(kc, (w_s * vx.astype(F32)).astype(BF16))
            return h, m_new

        ct_s[...] = jnp.zeros(ct_s.shape, F32)

        def fwd(c, m):
            h, m_new = step(c, m, hh, nh + hh, tril, CHUNK - 1)
            hf_s[_rows(c), :] = h
            return m_new
        lax.fori_loop(0, n_chunks, fwd, jnp.zeros((1, 1), F32))

        ct_s[...] = jnp.zeros(ct_s.shape, F32)

        def bwd(i, m):
            c = n_chunks - 1 - i
            h, m_new = step(c, m, 2 * nh + hh, 3 * nh + hh, triu, 0)
            h = h + hf_s[_rows(c), :]
            y = _rms(h, hg_ref[:, cs])
            xc = xc_s[_rows(c), cs].astype(F32)
            so = so_ref[0, 0, _rows(c), cs].astype(F32)
            out_ref[0, _rows(c), cs] = (so * (y + sk_ref[:, cs] * xc)).astype(BF16)
            return m_new
        lax.fori_loop(0, n_chunks, bwd, jnp.zeros((1, 1), F32))


def _mlstm(z_all, cw, cb, wq, wk, wv, wg, wgt, bg, bgt, hg, sk):
    _, b, s, d = z_all.shape
    dh = d // ML_HEADS
    ng = 4 * ML_HEADS

    def full(a):
        return pl.BlockSpec(a.shape, lambda i, _n=a.ndim: (0,) * _n)

    return pl.pallas_call(
        _mlstm_kernel,
        grid=(b,),
        in_specs=[
            pl.BlockSpec((1, 1, s, d), lambda i: (1, i, 0, 0)),
            pl.BlockSpec((1, 1, s, d), lambda i: (2, i, 0, 0)),
            full(cw), full(cb), full(wq), full(wk), full(wv), full(wg), full(wgt),
            full(bg), full(bgt), full(hg), full(sk),
        ],
        out_specs=pl.BlockSpec((1, s, d), lambda i: (i, 0, 0)),
        out_shape=jax.ShapeDtypeStruct((b, s, d), BF16),
        scratch_shapes=[
            pltpu.VMEM((s + 2 * CONV_PAD, d), F32),
            pltpu.VMEM((s, d), BF16),
            pltpu.VMEM((s, ng), F32),
            pltpu.VMEM((ng, s), F32),
            pltpu.VMEM((s, dh), BF16),
            pltpu.VMEM((s, dh), BF16),
            pltpu.VMEM((s, dh + CHUNK), BF16),
            pltpu.VMEM((s, dh), F32),
            pltpu.VMEM((dh, dh + CHUNK), F32),
        ],
        compiler_params=_cparams("arbitrary"),
        name="mlstm",
    )(z_all, z_all, cw, cb, wq, wk, wv, wg, wgt, bg, bgt, hg, sk)


def _merge_kernel(x_ref, ya_ref, yb_ref, ga_ref, gb_ref, wa_ref, wb_ref, wo_ref, out_ref):
    merged = (ga_ref[0].astype(F32) * _dot(ya_ref[0], wa_ref[...])
              + gb_ref[0].astype(F32) * _dot(yb_ref[...], wb_ref[...]))
    out_ref[...] = x_ref[...] + _dot(merged.astype(BF16), wo_ref[...])


def _merge(x2, z_all, yb2, wa, wb, wo, tm):
    t, d = x2.shape
    z3 = z_all.reshape(z_all.shape[0], t, d)

    def slab(k):
        return pl.BlockSpec((1, tm, d), lambda i, _k=k: (_k, i, 0))

    row = pl.BlockSpec((tm, d), lambda i: (i, 0))
    wspec = pl.BlockSpec((d, d), lambda i: (0, 0))
    return pl.pallas_call(
        _merge_kernel,
        grid=(t // tm,),
        in_specs=[row, slab(0), row, slab(3), slab(4), wspec, wspec, wspec],
        out_specs=row,
        out_shape=jax.ShapeDtypeStruct((t, d), F32),
        compiler_params=_cparams("arbitrary"),
        name="merge",
    )(x2, z3, yb2, z3, z3, wa, wb, wo)


def _peer_route_kernel(x_ref, g_ref, wq_ref, keys_ref, xpk_ref, idx_ref, gw_ref,
                       q_s, sv_s, si_s, cv_s, ce_s, it_s, gt_s):
    tm = x_ref.shape[0]
    dw = x_ref.shape[1] // 2
    k = PEER_TOPK
    big = 1e9

    xb = _rms(x_ref[...], g_ref[...]).astype(BF16)
    bits = lax.bitcast_convert_type(xb.astype(F32), I32)
    xpk_ref[...] = lax.shift_right_logical(bits[:, :dw], jnp.full((tm, dw), 16, I32)) | bits[:, dw:]
    q_s[...] = _dot(xb, wq_ref[...]).astype(BF16)

    def sub_topk(hp, carry):
        qh = q_s[:, pl.ds(pl.multiple_of(hp * CHUNK, CHUNK), CHUNK)]
        sc = _dot_nt(keys_ref[hp], qh)
        rowf = lax.broadcasted_iota(I32, sc.shape, 0).astype(F32)
        for j in range(k):
            m = jnp.max(sc, axis=0, keepdims=True)
            i = jnp.min(jnp.where(sc == m, rowf, big), axis=0, keepdims=True)
            sv_s[hp, j:j + 1, :] = m
            si_s[hp, j:j + 1, :] = i
            sc = jnp.where(rowf == i, -jnp.inf, sc)
        return carry
    lax.fori_loop(0, 2 * PEER_HEADS, sub_topk, 0)

    def head_topk(h, carry):
        sv0, sv1 = sv_s[2 * h], sv_s[2 * h + 1]
        si0, si1 = si_s[2 * h], si_s[2 * h + 1]
        half = k // 2
        cand = jnp.concatenate(
            [sv0[0:1, :] + sv1] + [sv0[a:a + 1, :] + sv1[0:half, :] for a in range(1, half)]
            + [sv0[half:k, :] + sv1[0:1, :]], axis=0)
        eid = jnp.concatenate(
            [si0[0:1, :] * float(CHUNK) + si1]
            + [si0[a:a + 1, :] * float(CHUNK) + si1[0:half, :] for a in range(1, half)]
            + [si0[half:k, :] * float(CHUNK) + si1[0:1, :]], axis=0)
        row = lax.broadcasted_iota(I32, cand.shape, 0)
        n_mid = k + half * (half - 1)
        flat = jnp.where(row < k, row,
                         jnp.where(row < n_mid, row + half * ((row - k) // half),
                                   k * (row - n_mid + half)))
        rowf = flat.astype(F32)
        for j in range(k):
            m = jnp.max(cand, axis=0, keepdims=True)
            i = jnp.min(jnp.where(cand == m, rowf, big), axis=0, keepdims=True)
            hit = rowf == i
            cv_s[j:j + 1, :] = m
            ce_s[j:j + 1, :] = jnp.sum(jnp.where(hit, eid, 0.0), axis=0, keepdims=True)
            cand = jnp.where(hit, -jnp.inf, cand)
        cv = cv_s[...]
        e = jnp.exp(cv - cv[0:1, :])
        rows = pl.ds(pl.multiple_of(h * k, k), k)
        gt_s[rows, :] = e / jnp.sum(e, axis=0, keepdims=True)
        it_s[rows, :] = ce_s[...]
        return carry
    lax.fori_loop(0, PEER_HEADS, head_topk, 0)

    idx_ref[...] = it_s[...].T.astype(I32)
    gw_ref[...] = gt_s[...].T


def _peer_route(x2, g, wq, keys, tm):
    t, d = x2.shape
    qd = wq.shape[1]
    nsel = PEER_HEADS * PEER_TOPK
    row = pl.BlockSpec((tm, d), lambda i: (i, 0))
    sel = pl.BlockSpec((tm, nsel), lambda i: (i, 0))
    return pl.pallas_call(
        _peer_route_kernel,
        grid=(t // tm,),
        in_specs=[
            row,
            pl.BlockSpec((1, d), lambda i: (0, 0)),
            pl.BlockSpec((d, qd), lambda i: (0, 0)),
            pl.BlockSpec(keys.shape, lambda i: (0, 0, 0)),
        ],
        out_specs=[pl.BlockSpec((tm, d // 2), lambda i: (i, 0)), sel, sel],
        out_shape=[
            jax.ShapeDtypeStruct((t, d // 2), I32),
            jax.ShapeDtypeStruct((t, nsel), I32),
            jax.ShapeDtypeStruct((t, nsel), F32),
        ],
        scratch_shapes=[
            pltpu.VMEM((tm, qd), BF16),
            pltpu.VMEM((2 * PEER_HEADS, PEER_TOPK, tm), F32),
            pltpu.VMEM((2 * PEER_HEADS, PEER_TOPK, tm), F32),
            pltpu.VMEM((PEER_TOPK, tm), F32),
            pltpu.VMEM((PEER_TOPK, tm), F32),
            pltpu.VMEM((nsel, tm), F32),
            pltpu.VMEM((nsel, tm), F32),
        ],
        compiler_params=_cparams("arbitrary"),
        name="peer_route",
    )(x2, g, wq, keys)


def _merge_route_kernel(x_ref, ya_ref, yb_ref, ga_ref, gb_ref, wa_ref, wb_ref, wo_ref,
                        g_ref, wq_ref, keys_ref, xout_ref, xpk_ref, idx_ref, gw_ref, *scratch):
    _merge_kernel(x_ref, ya_ref, yb_ref, ga_ref, gb_ref, wa_ref, wb_ref, wo_ref, xout_ref)
    _peer_route_kernel(xout_ref, g_ref, wq_ref, keys_ref, xpk_ref, idx_ref, gw_ref, *scratch)


def _merge_route(x2, z_all, yb2, wa, wb, wo, g, wq, keys, tm):
    t, d = x2.shape
    qd = wq.shape[1]
    nsel = PEER_HEADS * PEER_TOPK
    z3 = z_all.reshape(z_all.shape[0], t, d)

    def slab(k):
        return pl.BlockSpec((1, tm, d), lambda i, _k=k: (_k, i, 0))

    row = pl.BlockSpec((tm, d), lambda i: (i, 0))
    sel = pl.BlockSpec((tm, nsel), lambda i: (i, 0))
    wspec = pl.BlockSpec((d, d), lambda i: (0, 0))
    return pl.pallas_call(
        _merge_route_kernel,
        grid=(t // tm,),
        in_specs=[row, slab(0), row, slab(3), slab(4), wspec, wspec, wspec,
                  pl.BlockSpec((1, d), lambda i: (0, 0)),
                  pl.BlockSpec((d, qd), lambda i: (0, 0)),
                  pl.BlockSpec(keys.shape, lambda i: (0, 0, 0))],
        out_specs=[row, pl.BlockSpec((tm, d // 2), lambda i: (i, 0)), sel, sel],
        out_shape=[
            jax.ShapeDtypeStruct((t, d), F32),
            jax.ShapeDtypeStruct((t, d // 2), I32),
            jax.ShapeDtypeStruct((t, nsel), I32),
            jax.ShapeDtypeStruct((t, nsel), F32),
        ],
        scratch_shapes=[
            pltpu.VMEM((tm, qd), BF16),
            pltpu.VMEM((2 * PEER_HEADS, PEER_TOPK, tm), F32),
            pltpu.VMEM((2 * PEER_HEADS, PEER_TOPK, tm), F32),
            pltpu.VMEM((PEER_TOPK, tm), F32),
            pltpu.VMEM((PEER_TOPK, tm), F32),
            pltpu.VMEM((nsel, tm), F32),
            pltpu.VMEM((nsel, tm), F32),
        ],
        compiler_params=_cparams("arbitrary"),
        name="merge_route",
    )(x2, z3, yb2, z3, z3, wa, wb, wo, g, wq, keys)


SC_LANES = 16
SC_TOK_BLOCK = 32
SC_RING = 8
SC_AHEAD = 7


def _pack_rows(tab):
    d = tab.shape[1]
    bits = lax.bitcast_convert_type(tab.astype(BF16), jnp.uint16).astype(jnp.uint32)
    return lax.bitcast_convert_type(bits[:, :d // 2] | (bits[:, d // 2:] << 16), I32)


def _unpack_words(w):
    lo = lax.bitcast_convert_type(lax.shift_left(w, jnp.full(w.shape, 16, I32)), F32)
    hi = lax.bitcast_convert_type(w & jnp.full(w.shape, -65536, I32), F32)
    return lo, hi


def _mul2_packed(a0, b0, a1, b1):
    p = (plsc.bitcast(a0, BF16) * plsc.bitcast(b0, BF16)
         + plsc.bitcast(a1, BF16) * plsc.bitcast(b1, BF16))
    return _unpack_words(plsc.bitcast(p, I32))


def _round_to_packed_pair(x):
    bits = lax.bitcast_convert_type(x, I32)
    odd = lax.shift_right_logical(bits, jnp.full(x.shape, 16, I32)) & jnp.full(x.shape, 1, I32)
    hi = (bits + jnp.full(x.shape, 0x7FFF, I32) + odd) & jnp.full(x.shape, -65536, I32)
    return hi | lax.shift_right_logical(hi, jnp.full(x.shape, 16, I32))


def _peer_experts(u_pk, v_pk, idx, gw, xn_pk, xres):
    t, d = xres.shape
    k = PEER_TOPK
    nh = PEER_HEADS
    dw = d // 2
    n_steps = 2 * nh
    assert k == SC_LANES and n_steps % SC_RING == 0 and SC_AHEAD < SC_RING and dw % (2 * SC_LANES) == 0
    info = plsc.get_sparse_core_info()
    nc, ns = info.num_cores, info.num_subcores
    nw = nc * ns
    tb = SC_TOK_BLOCK
    assert t % (nw * tb) == 0
    tpw = t // nw
    n_chunks = dw // SC_LANES
    mesh = plsc.VectorSubcoreMesh(core_axis_name="c", subcore_axis_name="s")

    @functools.partial(
        pl.kernel,
        mesh=mesh,
        out_type=jax.ShapeDtypeStruct((t, d), F32),
        scratch_types=[
            pltpu.VMEM((tb, nh * k), I32),
            pltpu.VMEM((tb, nh * k), F32),
            pltpu.VMEM((tb, dw), I32),
            pltpu.VMEM((tb, d), F32),
            pltpu.VMEM((SC_RING, k, dw), I32),
            pltpu.VMEM((k, SC_LANES), F32),
            pltpu.VMEM((nh, k), F32),
            pltpu.VMEM((nh, k), I32),
            pltpu.SemaphoreType.DMA((SC_RING,)),
            pltpu.SemaphoreType.DMA((4,)),
        ],
        compiler_params=pltpu.CompilerParams(needs_layout_passes=False),
        name="peer_experts",
    )
    def run(u_hbm, v_hbm, idx_hbm, gw_hbm, xn_hbm, xres_hbm, out_hbm,
            idx_v, gw_v, xn_v, out_v, ring, red_v, act_v, w_v, sems, in_sems):
        wid = lax.axis_index("s") * nc + lax.axis_index("c")
        tok0 = wid * tpw
        lane = lax.iota(I32, SC_LANES)

        def gather(tok, step):
            tab = u_hbm if step < nh else v_hbm
            slot = step % SC_RING
            return pltpu.make_async_copy(
                tab.at[idx_v.at[tok, pl.ds((step % nh) * k, k)]], ring.at[slot], sems.at[slot])

        @pl.loop(0, tpw // tb)
        def _(blk):
            base = tok0 + blk * tb
            loads = [
                pltpu.make_async_copy(src.at[pl.ds(base, tb)], dst, in_sems.at[i])
                for i, (src, dst) in enumerate(
                    ((idx_hbm, idx_v), (gw_hbm, gw_v), (xn_hbm, xn_v), (xres_hbm, out_v)))]
            for cp in loads:
                cp.start()
            loads[0].wait()
            for step in range(SC_AHEAD):
                gather(0, step).start()
            for cp in loads[1:]:
                cp.wait()

            @pl.loop(0, tb)
            def _(tok):
                for step in range(n_steps):
                    h = step % nh
                    slot = step % SC_RING
                    ahead = step + SC_AHEAD
                    if ahead < n_steps:
                        gather(tok, ahead).start()
                    else:
                        @pl.when(tok + 1 < tb)
                        def _():
                            gather(tok + 1, ahead - n_steps).start()

                    if step == nh:
                        for hh in range(nh):
                            act = act_v[hh, :]
                            z = GELU_C * (act + 0.044715 * act * act * act)
                            tanh_z = 1.0 - 2.0 / (jnp.exp(2.0 * z) + 1.0)
                            w_v[hh, :] = _round_to_packed_pair(
                                gw_v[tok, pl.ds(hh * k, k)] * (0.5 * act * (1.0 + tanh_z)))

                    gather(tok, step).wait()

                    if step < nh:
                        def dot_rows(c, accs):
                            w0 = pl.multiple_of(c * 2 * SC_LANES, 2 * SC_LANES)
                            w1 = w0 + SC_LANES
                            x0 = xn_v[tok, pl.ds(w0, SC_LANES)]
                            x1 = xn_v[tok, pl.ds(w1, SC_LANES)]
                            out = []
                            for r in range(k):
                                lo, hi = _mul2_packed(ring[slot, r, pl.ds(w0, SC_LANES)], x0,
                                                      ring[slot, r, pl.ds(w1, SC_LANES)], x1)
                                out.append(accs[r] + (lo + hi))
                            return tuple(out)
                        accs = lax.fori_loop(
                            0, n_chunks // 2, dot_rows,
                            tuple(jnp.zeros((SC_LANES,), F32) for _ in range(k)))
                        for r in range(k):
                            red_v[r, :] = accs[r]
                        cols = [plsc.load_gather(red_v, [lane, jnp.full((SC_LANES,), jj, I32)])
                                for jj in range(SC_LANES)]
                        while len(cols) > 1:
                            cols = [cols[i] + cols[i + 1] for i in range(0, len(cols), 2)]
                        act_v[h, :] = cols[0]
                    else:
                        wv = w_v[h, :]
                        ws = [wv.at[jnp.full((SC_LANES,), r, I32)].get(mode="promise_in_bounds")
                              for r in range(k)]

                        @plsc.parallel_loop(0, n_chunks)
                        def _(c):
                            wo = pl.multiple_of(c * SC_LANES, SC_LANES)
                            los, his = [], []
                            for r in range(0, k, 2):
                                lo, hi = _mul2_packed(ring[slot, r, pl.ds(wo, SC_LANES)], ws[r],
                                                      ring[slot, r + 1, pl.ds(wo, SC_LANES)], ws[r + 1])
                                los.append(lo)
                                his.append(hi)
                            while len(los) > 1:
                                los = [los[i] + los[i + 1] for i in range(0, len(los), 2)]
                                his = [his[i] + his[i + 1] for i in range(0, len(his), 2)]
                            plsc.addupdate(out_v.at[tok, pl.ds(wo, SC_LANES)], los[0])
                            plsc.addupdate(out_v.at[tok, pl.ds(dw + wo, SC_LANES)], his[0])

            pltpu.sync_copy(out_v, out_hbm.at[pl.ds(base, tb)])

    return run(u_pk, v_pk, idx, gw, xn_pk, xres)


def _final_norm_kernel(x_ref, g_ref, out_ref):
    out_ref[...] = _rms(x_ref[...], g_ref[...])


def _final_norm(x2, g, tm):
    t, d = x2.shape
    row = pl.BlockSpec((tm, d), lambda i: (i, 0))
    return pl.pallas_call(
        _final_norm_kernel,
        grid=(t // tm,),
        in_specs=[row, pl.BlockSpec((1, d), lambda i: (0, 0))],
        out_specs=row,
        out_shape=jax.ShapeDtypeStruct((t, d), F32),
        compiler_params=_cparams("arbitrary"),
        name="final_norm",
    )(x2, g)


def _row_tile(t, want):
    tm = min(want, t)
    assert t % tm == 0
    return tm


def kernel(x, norm_mix_g, w_in, gm_norm_g, gm_w_s, gm_b_s, ml_conv_w, ml_conv_b, ml_w_q, ml_w_k, ml_w_v, ml_w_gate, ml_b_gate, ml_head_g, ml_skip, w_branch_a, w_branch_b, w_out, norm_ffn_g, peer_w_query, peer_sub_keys, peer_u, peer_v, final_g):
    b, s, d = x.shape
    depth = w_in.shape[0]
    assert s % CHUNK == 0 and d % (ML_HEADS * CHUNK) == 0 and w_in.shape[2] == N_IN_SEG * d
    n_groups = BATCH_GROUPS if b % BATCH_GROUPS == 0 else 1
    bg = b // n_groups
    t = bg * s
    tm_merge = _row_tile(t, 512)
    tm_route = _row_tile(t, 256)
    parts = [x[i * bg:(i + 1) * bg].reshape(t, d) for i in range(n_groups)]
    for l in range(depth):
        w_in_b = w_in[l].astype(BF16)
        w_s_b = gm_w_s[l].astype(BF16)
        wq_b, wk_b, wv_b = (ml_w_q[l].astype(BF16), ml_w_k[l].astype(BF16), ml_w_v[l].astype(BF16))
        wg = ml_w_gate[l].astype(BF16)
        wa_b, wb_b, wo_b = (w_branch_a[l].astype(BF16), w_branch_b[l].astype(BF16),
                            w_out[l].astype(BF16))
        wquery_b = peer_w_query[l].astype(BF16)
        keys = peer_sub_keys[l].reshape(2 * PEER_HEADS, CHUNK, -1).astype(BF16)
        u_pk, v_pk = _pack_rows(peer_u[l]), _pack_rows(peer_v[l])
        for p in range(n_groups):
            x2 = parts[p]
            z_all = _in_proj(
                x2.reshape(bg, s, d), norm_mix_g[l].reshape(1, d), w_in_b,
                gm_norm_g[l].reshape(1, d), w_s_b, gm_b_s[l].T)
            yb = _mlstm(
                z_all, ml_conv_w[l], ml_conv_b[l].reshape(1, d), wq_b, wk_b, wv_b,
                wg, wg.T, ml_b_gate[l].reshape(1, -1), ml_b_gate[l].reshape(-1, 1),
                ml_head_g[l].reshape(1, d), ml_skip[l].reshape(1, d))
            x2, xn_pk, idx, gw = _merge_route(x2, z_all, yb.reshape(t, d), wa_b, wb_b, wo_b,
                                              norm_ffn_g[l].reshape(1, d), wquery_b, keys, tm_route)
            parts[p] = _peer_experts(u_pk, v_pk, idx, gw, xn_pk, x2)
    outs = [_final_norm(p, final_g.reshape(1, d), tm_merge) for p in parts]
    return jnp.concatenate(outs, axis=0).reshape(b, s, d)
```
